```python
import math
import jax, jax.numpy as jnp
from jax import lax
import numpy as np

D_MODEL = 1024
BATCH = 8
SEQ = 2048
DEPTH = 1
DEC_BATCH = 32
DEC_SEQ = 8
PAST_LEN = 16384
PAGE_SIZE = 128

H_A = 4
DK_A = 64
DV_A = 128
GATE_RANK = 16
GATE_TAU = 16.0
GLA_CHUNK = 64
H_B = 4
Q_LORA = 384
KV_LORA = 256
QK_NOPE = 128
QK_ROPE = 64
V_B = 128
ROPE_BASE = 10000.0
Q_BLOCK = 128
MIX_WIDTH = H_A * DV_A + H_B * V_B
D_FF = 2816
CONV_W = 3
EPS = 1e-6
COLS = (H_A * DK_A, H_A * DK_A, H_A * DV_A, H_A * DV_A, GATE_RANK, Q_LORA, KV_LORA, QK_ROPE)
IN_COLS = sum(COLS)

kernel_name = "hymba_gla_mla_convffn_step"


def rmsnorm(x, g):
    xf = x.astype(jnp.float32)
    y = xf * lax.rsqrt(jnp.mean(xf * xf, axis=-1, keepdims=True) + EPS)
    return (y * g.astype(jnp.float32)).astype(x.dtype)


def rope(x, pos):
    r = x.shape[-1]
    inv = ROPE_BASE ** (-jnp.arange(0, r, 2, dtype=jnp.float32) / r)
    ang = pos.astype(jnp.float32)[:, None] * inv[None, :]
    cos = jnp.cos(ang)[None, :, None, :]
    sin = jnp.sin(ang)[None, :, None, :]
    xf = x.astype(jnp.float32)
    x1, x2 = xf[..., : r // 2], xf[..., r // 2:]
    out = jnp.concatenate([x1 * cos - x2 * sin, x1 * sin + x2 * cos], axis=-1)
    return out.astype(x.dtype)


def gla_chunked(q, k, v, log_a, s0):
    B, T, H, DK = q.shape
    DV = v.shape[-1]
    C = math.gcd(T, GLA_CHUNK)
    N = T // C

    def to_chunks(a):
        return a.astype(jnp.float32).reshape(B, N, C, H, a.shape[-1]).transpose(1, 0, 2, 3, 4)

    tri = jnp.tril(jnp.ones((C, C), dtype=bool))[None, :, :, None, None]

    def step(S, inp):
        qc, kc, vc, gc = inp
        b = jnp.cumsum(gc, axis=1)
        o_inter = jnp.einsum('bchk,bhkv->bchv', qc * jnp.exp(b), S)
        diff = jnp.where(tri, b[:, :, None] - b[:, None, :], -jnp.inf)
        att = jnp.einsum('bthk,bshk,btshk->bhts', qc, kc, jnp.exp(diff))
        o_intra = jnp.einsum('bhts,bshv->bthv', att, vc)
        bl = b[:, -1]
        S_new = S * jnp.exp(bl)[..., None] + jnp.einsum('bshk,bshv->bhkv', kc * jnp.exp(bl[:, None] - b), vc)
        return S_new, o_inter + o_intra

    S_fin, o = lax.scan(step, s0.astype(jnp.float32), (to_chunks(q), to_chunks(k), to_chunks(v), to_chunks(log_a)))
    o = o.transpose(1, 0, 2, 3, 4).reshape(B, T, H, DV).astype(v.dtype)
    return o, S_fin


def mla_attend(q_lat, q_rope, kv, kr, q_pos, k_pos):
    B, T, H, L = q_lat.shape
    R = q_rope.shape[-1]
    scale = (QK_NOPE + QK_ROPE) ** -0.5

    def block(args):
        ql, qr, qp = args
        s = jnp.einsum('bthl,bsl->bhts', ql, kv) + jnp.einsum('bthr,bsr->bhts', qr, kr)
        s = s.astype(jnp.float32) * scale
        s = jnp.where(k_pos[None, :] <= qp[:, None], s, -jnp.inf)
        p = jax.nn.softmax(s, axis=-1).astype(kv.dtype)
        return jnp.einsum('bhts,bsl->bthl', p, kv)

    blk = Q_BLOCK if T % Q_BLOCK == 0 else T
    nb = T // blk
    if nb == 1:
        return block((q_lat, q_rope, q_pos))
    qlb = q_lat.reshape(B, nb, blk, H, L).transpose(1, 0, 2, 3, 4)
    qrb = q_rope.reshape(B, nb, blk, H, R).transpose(1, 0, 2, 3, 4)
    qpb = q_pos.reshape(nb, blk)
    out = lax.map(block, (qlb, qrb, qpb))
    return out.transpose(1, 0, 2, 3, 4).reshape(B, T, H, L)


def trunk_layer(x, c, pos, gla_s0, conv_buf, past_lat, past_kr, lw):
    B, T, _ = x.shape
    mod = jax.nn.silu(c) @ lw['w_ada'] + lw['b_ada']
    sh1, sc1, gt1, sh2, sc2, gt2 = [m[:, None, :] for m in jnp.split(mod, 6, axis=-1)]

    h = rmsnorm(x, lw['g_pre_mix']) * (1.0 + sc1) + sh1
    z = h @ lw['w_in']
    offs = np.cumsum(COLS)[:-1].tolist()
    q_a, k_a, v_a, r_a, gr_a, cq, ckv, kr = jnp.split(z, offs, axis=-1)

    q_a = q_a.reshape(B, T, H_A, DK_A) * (DK_A ** -0.5)
    k_a = k_a.reshape(B, T, H_A, DK_A)
    v_a = v_a.reshape(B, T, H_A, DV_A)
    log_a = jax.nn.log_sigmoid((gr_a @ lw['w_gla_a2'] + lw['b_gla_a']).astype(jnp.float32)) / GATE_TAU
    log_a = log_a.reshape(B, T, H_A, DK_A)
    o_a, s_a = gla_chunked(q_a, k_a, v_a, log_a, gla_s0)
    o_a = rmsnorm(o_a, lw['g_gla_out']) * jax.nn.silu(r_a.reshape(B, T, H_A, DV_A))
    o_a = o_a.reshape(B, T, H_A * DV_A)

    cq = rmsnorm(cq, lw['g_mla_q'])
    qb = (cq @ lw['w_mla_uq']).reshape(B, T, H_B, QK_NOPE + QK_ROPE)
    q_nope, q_rope = qb[..., :QK_NOPE], rope(qb[..., QK_NOPE:], pos)
    ckv = rmsnorm(ckv, lw['g_mla_kv'])
    kr = rope(kr[:, :, None, :], pos)[:, :, 0, :]
    q_lat = jnp.einsum('bthn,lhn->bthl', q_nope, lw['w_mla_uk'])
    if past_lat is None:
        kv_all, kr_all, k_pos = ckv, kr, pos
    else:
        kv_all = jnp.concatenate([past_lat, ckv], axis=1)
        kr_all = jnp.concatenate([past_kr, kr], axis=1)
        k_pos = jnp.concatenate([jnp.arange(past_lat.shape[1], dtype=jnp.int32), pos])
    o_lat = mla_attend(q_lat, q_rope, kv_all, kr_all, pos, k_pos)
    o_b = jnp.einsum('bthl,lhv->bthv', o_lat, lw['w_mla_uv'])
    o_b = rmsnorm(o_b, lw['g_mla_out']).reshape(B, T, H_B * V_B)

    mix = jnp.concatenate([o_a, o_b], axis=-1) @ lw['w_o']
    x = x + gt1 * rmsnorm(mix, lw['g_post_mix'])

    h2 = rmsnorm(x, lw['g_pre_ffn']) * (1.0 + sc2) + sh2
    u = h2 @ lw['w_ffn_up']
    if conv_buf is None:
        conv_buf = jnp.zeros((B, CONV_W - 1, u.shape[-1]), dtype=u.dtype)
    padded = jnp.concatenate([conv_buf.astype(u.dtype), u], axis=1)
    w_c = lw['w_ffn_conv']
    uc = sum(w_c[j] * padded[:, j:j + T] for j in range(CONV_W)) + lw['b_ffn_conv']
    a_up, g_up = uc[..., :D_FF], uc[..., D_FF:]
    y = (a_up * jax.nn.gelu(g_up)) @ lw['w_ffn_down']
    x = x + gt2 * rmsnorm(y, lw['g_post_ffn'])
    new_conv = padded[:, -(CONV_W - 1):]
    return x, ckv, kr, s_a, new_conv


def setup_inputs(seed: int = 0) -> dict:
    key = jax.random.key(seed)
    ks = jax.random.split(key, 40)
    f32 = jnp.float32
    n_pages = PAST_LEN // PAGE_SIZE
    n_used = DEC_BATCH * n_pages
    n_phys = n_used + n_used // 4

    def nrm(k, shape, scale):
        return jax.random.normal(k, shape, f32) * scale

    def gain(k, shape):
        return 1.0 + 0.1 * jax.random.normal(k, shape, f32)

    page_table = jax.random.permutation(ks[0], n_phys)[:n_used].reshape(DEC_BATCH, n_pages).astype(jnp.int32)
    return {
        'x_prompt': nrm(ks[1], (BATCH, SEQ, D_MODEL), 1.0),
        'x_sample': nrm(ks[2], (DEC_BATCH, DEC_SEQ, D_MODEL), 1.0),
        'c_prompt': nrm(ks[3], (BATCH, D_MODEL), 1.0),
        'c_sample': nrm(ks[4], (DEC_BATCH, D_MODEL), 1.0),
        'cache_latent': nrm(ks[5], (DEPTH, n_phys, PAGE_SIZE, KV_LORA), 1.0),
        'cache_krope': nrm(ks[6], (DEPTH, n_phys, PAGE_SIZE, QK_ROPE), 1.0),
        'state_gla': nrm(ks[7], (DEPTH, DEC_BATCH, H_A, DK_A, DV_A), 0.5),
        'state_conv': nrm(ks[8], (DEPTH, DEC_BATCH, CONV_W - 1, 2 * D_FF), 1.0),
        'page_table': page_table,
        'w_ada': nrm(ks[9], (DEPTH, D_MODEL, 6 * D_MODEL), 0.5 * D_MODEL ** -0.5),
        'b_ada': nrm(ks[10], (DEPTH, 6 * D_MODEL), 0.01),
        'g_pre_mix': gain(ks[11], (DEPTH, D_MODEL)),
        'g_post_mix': gain(ks[12], (DEPTH, D_MODEL)),
        'g_pre_ffn': gain(ks[13], (DEPTH, D_MODEL)),
        'g_post_ffn': gain(ks[14], (DEPTH, D_MODEL)),
        'w_in': nrm(ks[15], (DEPTH, D_MODEL, IN_COLS), D_MODEL ** -0.5),
        'w_gla_a2': nrm(ks[16], (DEPTH, GATE_RANK, H_A * DK_A), GATE_RANK ** -0.5),
        'b_gla_a': nrm(ks[17], (DEPTH, H_A * DK_A), 0.01),
        'g_gla_out': gain(ks[18], (DEPTH, DV_A)),
        'g_mla_q': gain(ks[19], (DEPTH, Q_LORA)),
        'g_mla_kv': gain(ks[20], (DEPTH, KV_LORA)),
        'w_mla_uq': nrm(ks[21], (DEPTH, Q_LORA, H_B * (QK_NOPE + QK_ROPE)), Q_LORA ** -0.5),
        'w_mla_uk': nrm(ks[22], (DEPTH, KV_LORA, H_B, QK_NOPE), KV_LORA ** -0.5),
        'w_mla_uv': nrm(ks[23], (DEPTH, KV_LORA, H_B, V_B), KV_LORA ** -0.5),
        'g_mla_out': gain(ks[24], (DEPTH, V_B)),
        'w_o': nrm(ks[25], (DEPTH, MIX_WIDTH, D_MODEL), MIX_WIDTH ** -0.5),
        'w_ffn_up': nrm(ks[26], (DEPTH, D_MODEL, 2 * D_FF), D_MODEL ** -0.5),
        'w_ffn_conv': nrm(ks[27], (DEPTH, CONV_W, 2 * D_FF), CONV_W ** -0.5),
        'b_ffn_conv': nrm(ks[28], (DEPTH, 2 * D_FF), 0.01),
        'w_ffn_down': nrm(ks[29], (DEPTH, D_FF, D_MODEL), D_FF ** -0.5),
    }


def reference(x_prompt, x_sample, c_prompt, c_sample, cache_latent, cache_krope, state_gla, state_conv,
              page_table, w_ada, b_ada, g_pre_mix, g_post_mix, g_pre_ffn, g_post_ffn, w_in, w_gla_a2,
              b_gla_a, g_gla_out, g_mla_q, g_mla_kv, w_mla_uq, w_mla_uk, w_mla_uv, g_mla_out, w_o,
              w_ffn_up, w_ffn_conv, b_ffn_conv, w_ffn_down):
    B, T = x_prompt.shape[0], x_prompt.shape[1]
    Bd, Td = x_sample.shape[0], x_sample.shape[1]
    n_pages = page_table.shape[1]
    past_len = n_pages * cache_latent.shape[2]
    pos_p = jnp.arange(T, dtype=jnp.int32)
    pos_s = past_len + jnp.arange(Td, dtype=jnp.int32)

    hp, hs = x_prompt, x_sample
    lat_p, kr_p, gla_p, conv_p = [], [], [], []
    lat_s, kr_s, gla_s, conv_s = [], [], [], []
    for l in range(DEPTH):
        lw = dict(w_ada=w_ada[l], b_ada=b_ada[l], g_pre_mix=g_pre_mix[l], g_post_mix=g_post_mix[l],
                  g_pre_ffn=g_pre_ffn[l], g_post_ffn=g_post_ffn[l], w_in=w_in[l], w_gla_a2=w_gla_a2[l],
                  b_gla_a=b_gla_a[l], g_gla_out=g_gla_out[l], g_mla_q=g_mla_q[l], g_mla_kv=g_mla_kv[l],
                  w_mla_uq=w_mla_uq[l], w_mla_uk=w_mla_uk[l], w_mla_uv=w_mla_uv[l], g_mla_out=g_mla_out[l],
                  w_o=w_o[l], w_ffn_up=w_ffn_up[l], w_ffn_conv=w_ffn_conv[l], b_ffn_conv=b_ffn_conv[l],
                  w_ffn_down=w_ffn_down[l])
        s0 = jnp.zeros((B, H_A, DK_A, DV_A), dtype=jnp.float32)
        hp, a1, a2, a3, a4 = trunk_layer(hp, c_prompt, pos_p, s0, None, None, None, lw)
        lat_p.append(a1); kr_p.append(a2); gla_p.append(a3); conv_p.append(a4)
        past_lat = cache_latent[l][page_table].reshape(Bd, past_len, KV_LORA)
        past_kr = cache_krope[l][page_table].reshape(Bd, past_len, QK_ROPE)
        hs, b1, b2, b3, b4 = trunk_layer(hs, c_sample, pos_s, state_gla[l], state_conv[l], past_lat, past_kr, lw)
        lat_s.append(b1); kr_s.append(b2); gla_s.append(b3); conv_s.append(b4)

    return (hp, hs, jnp.stack(lat_p), jnp.stack(kr_p), jnp.stack(gla_p), jnp.stack(conv_p),
            jnp.stack(lat_s), jnp.stack(kr_s), jnp.stack(gla_s), jnp.stack(conv_s))
```

```python
import functools
import math

import jax
import jax.numpy as jnp
import numpy as np
from jax import lax
from jax.experimental import pallas as pl
from jax.experimental.pallas import tpu as pltpu

F32 = jnp.float32
BF16 = jnp.bfloat16

EPS = 1e-6
H_A, DK_A, DV_A = 4, 64, 128
GATE_RANK = 16
GATE_TAU = 16.0
H_B = 4
Q_LORA, KV_LORA = 384, 256
QK_NOPE, QK_ROPE, V_B = 128, 64, 128
ROPE_BASE = 10000.0
CONV_W = 3

HK = H_A * DK_A
HV = H_A * DV_A
QCAT = KV_LORA + 2 * QK_ROPE
LANE = 128
SUBLANE = 8

_O_QA, _O_KA, _O_VA, _O_RA = 0, HK, 2 * HK, 2 * HK + HV
_O_CQ = 2 * HK + 2 * HV
_O_CKV = _O_CQ + Q_LORA
_O_KRR = _O_CKV + KV_LORA
_O_GR = _O_KRR + 2 * QK_ROPE
IN_EXT = _O_GR + LANE

VMEM_LIMIT = 56 * 1024 * 1024
GLA_EXP_CLAMP = 80.0


def _mm(a, b):
    return jnp.dot(a, b, preferred_element_type=F32)


def _mm_nt(a, b):
    return lax.dot_general(a, b, (((1,), (1,)), ((), ())), preferred_element_type=F32)


def _rms(x, g):
    return x * lax.rsqrt(jnp.mean(x * x, axis=-1, keepdims=True) + EPS) * g


def _silu(x):
    return x / (1.0 + jnp.exp(-x))


def _log_sigmoid(x):
    return -(jnp.maximum(-x, 0.0) + jnp.log(1.0 + jnp.exp(-jnp.abs(x))))


def _split3(x):
    x1 = x.astype(BF16)
    r1 = x - x1.astype(F32)
    x2 = r1.astype(BF16)
    x3 = (r1 - x2.astype(F32)).astype(BF16)
    return x1, x2, x3


def _cparams(sem):
    return pltpu.CompilerParams(dimension_semantics=sem, vmem_limit_bytes=VMEM_LIMIT)


def _const_spec(shape):
    nd = len(shape)
    return pl.BlockSpec(shape, lambda *_: (0,) * nd)


def _ada_kernel(c_ref, w_ref, b_ref, o_ref):
    s = _silu(c_ref[...]).astype(BF16)
    o_ref[...] = _mm(s, w_ref[...].astype(BF16)) + b_ref[...]


def _ada(c_all, w_ada, b_ada):
    n, d = c_all.shape
    nout = w_ada.shape[1]
    tn = d
    return pl.pallas_call(
        _ada_kernel,
        out_shape=jax.ShapeDtypeStruct((n, nout), F32),
        grid=(nout // tn,),
        in_specs=[pl.BlockSpec((n, d), lambda j: (0, 0)),
                  pl.BlockSpec((d, tn), lambda j: (0, j)),
                  pl.BlockSpec((1, tn), lambda j: (0, j))],
        out_specs=pl.BlockSpec((n, tn), lambda j: (0, j)),
        compiler_params=_cparams(("arbitrary",)),
        name="ada",
    )(c_all, w_ada, b_ada.reshape(1, nout))


def _rope_sum(a3, cs):
    bb, tt, w = a3.shape
    a = (a3 * cs[None]).reshape(bb * tt, w)
    return a + pltpu.roll(a, QK_ROPE, axis=1)


def _inproj_kernel(x_ref, mod_ref, gpre_ref, win_ref, wkT_ref, wa2_ref, ba_ref, wa2T_ref, baT_ref,
                   gq_ref, gkv_ref, wuq_ref, wukT_ref, cs_ref,
                   qa_ref, ka_ref, va_ref, ra_ref, la_ref, kT_ref, laT_ref,
                   qcat_ref, kcat_ref, lat_ref, kr_ref):
    bb, tt, d = x_ref.shape
    m = bb * tt
    x = x_ref[...]
    mod = mod_ref[...]
    sh1, sc1 = mod[:, :, 0:d], mod[:, :, d:2 * d]
    h = _rms(x, gpre_ref[...]) * (1.0 + sc1) + sh1
    h = h.reshape(m, d).astype(BF16)
    z = _mm(h, win_ref[...])

    qa_ref[...] = (z[:, _O_QA:_O_QA + HK] * (DK_A ** -0.5)).reshape(bb, tt, HK)
    ka_ref[...] = z[:, _O_KA:_O_KA + HK].reshape(bb, tt, HK)
    va_ref[...] = z[:, _O_VA:_O_VA + HV].astype(va_ref.dtype).reshape(bb, tt, HV)
    ra_ref[...] = z[:, _O_RA:_O_RA + HV].astype(ra_ref.dtype).reshape(bb, tt, HV)
    gr = z[:, _O_GR:_O_GR + LANE].astype(BF16)
    la = _log_sigmoid(_mm(gr, wa2_ref[...]) + ba_ref[...]) * (1.0 / GATE_TAU)
    la_ref[...] = la.reshape(bb, tt, HK)
    kT_ref[...] = _mm_nt(wkT_ref[...], h)
    laT_ref[...] = _log_sigmoid(_mm_nt(wa2T_ref[...], gr) + baT_ref[...]) * (1.0 / GATE_TAU)

    cs = cs_ref[...]
    ckv = _rms(z[:, _O_CKV:_O_CKV + KV_LORA], gkv_ref[...])
    lat_ref[...] = ckv.reshape(bb, tt, KV_LORA)
    krs = _rope_sum(z[:, _O_KRR:_O_KRR + 2 * QK_ROPE].reshape(bb, tt, 2 * QK_ROPE), cs)
    kr_ref[...] = krs[:, 0:QK_ROPE].reshape(bb, tt, QK_ROPE)
    lane = lax.broadcasted_iota(jnp.int32, (m, 2 * QK_ROPE), 1)
    kcat_ref[:, :, 0:KV_LORA] = ckv.astype(kcat_ref.dtype).reshape(bb, tt, KV_LORA)
    kcat_ref[:, :, KV_LORA:QCAT] = (
        jnp.where(lane < QK_ROPE, krs, 0.0).astype(kcat_ref.dtype).reshape(bb, tt, 2 * QK_ROPE))

    cq = _rms(z[:, _O_CQ:_O_CQ + Q_LORA], gq_ref[...]).astype(BF16)
    qb = _mm(cq, wuq_ref[...])
    for hh in range(H_B):
        qn = qb[:, hh * QK_NOPE:(hh + 1) * QK_NOPE].astype(BF16)
        qlat = _mm(qn, wukT_ref[hh])
        qcat_ref[:, hh, :, 0:KV_LORA] = qlat.astype(qcat_ref.dtype).reshape(bb, tt, KV_LORA)
        o = H_B * QK_NOPE + hh * 2 * QK_ROPE
        qrs = _rope_sum(qb[:, o:o + 2 * QK_ROPE].reshape(bb, tt, 2 * QK_ROPE), cs)
        qcat_ref[:, hh, :, KV_LORA:QCAT] = qrs.astype(qcat_ref.dtype).reshape(bb, tt, 2 * QK_ROPE)


def _inproj(x, mod3, cs, w, *, bb, tt, act_dtype):
    nb, t, d = x.shape
    grid = (nb // bb, t // tt)
    if bb == 1:
        t_shape = jax.ShapeDtypeStruct((nb, HK, t), F32)
        tsp = pl.BlockSpec((None, HK, tt), lambda i, j: (i, 0, j))
    else:
        t_shape = jax.ShapeDtypeStruct((HK, nb * t), F32)
        tsp = pl.BlockSpec((HK, nb * t), lambda i, j: (0, 0))
    tok = lambda width: pl.BlockSpec((bb, tt, width), lambda i, j: (i, j, 0))
    out_shape = (
        jax.ShapeDtypeStruct((nb, t, HK), F32),
        jax.ShapeDtypeStruct((nb, t, HK), F32),
        jax.ShapeDtypeStruct((nb, t, HV), act_dtype),
        jax.ShapeDtypeStruct((nb, t, HV), act_dtype),
        jax.ShapeDtypeStruct((nb, t, HK), F32),
        t_shape,
        t_shape,
        jax.ShapeDtypeStruct((nb, H_B, t, QCAT), act_dtype),
        jax.ShapeDtypeStruct((nb, t, QCAT), act_dtype),
        jax.ShapeDtypeStruct((nb, t, KV_LORA), F32),
        jax.ShapeDtypeStruct((nb, t, QK_ROPE), F32),
    )
    out_specs = (tok(HK), tok(HK), tok(HV), tok(HV), tok(HK), tsp, tsp,
                 pl.BlockSpec((bb, H_B, tt, QCAT), lambda i, j: (i, 0, j, 0)),
                 tok(QCAT), tok(KV_LORA), tok(QK_ROPE))
    in_specs = [
        tok(d),
        pl.BlockSpec((bb, 1, mod3.shape[2]), lambda i, j: (i, 0, 0)),
        _const_spec((1, d)),
        _const_spec(w["win"].shape), _const_spec(w["wkT"].shape),
        _const_spec(w["wa2"].shape), _const_spec((1, HK)),
        _const_spec(w["wa2T"].shape), _const_spec((HK, 1)),
        _const_spec((1, Q_LORA)), _const_spec((1, KV_LORA)),
        _const_spec(w["wuq"].shape), _const_spec(w["wukT"].shape),
        pl.BlockSpec((tt, 2 * QK_ROPE), lambda i, j: (j, 0)),
    ]
    return pl.pallas_call(
        _inproj_kernel, out_shape=out_shape, grid=grid, in_specs=in_specs, out_specs=out_specs,
        compiler_params=_cparams(("arbitrary", "arbitrary")), name="inproj",
    )(x, mod3, w["g_pre_mix"], w["win"], w["wkT"], w["wa2"], w["ba"], w["wa2T"], w["baT"],
      w["g_mla_q"], w["g_mla_kv"], w["wuq"], w["wukT"], cs)


def _gla_kernel(q_ref, k_ref, la_ref, kT_ref, laT_ref, v_ref, r_ref, s0_ref, g_ref,
                o_ref, sfin_ref, s_ref, *, chunk, sub):
    j = pl.program_id(1)
    c_, r_ = chunk, sub
    nsub = c_ // r_
    tc = q_ref.shape[1]

    @pl.when(j == 0)
    def _():
        s_ref[...] = s0_ref[0]

    row = lax.broadcasted_iota(jnp.int32, (2 * c_, c_), 0)
    col = lax.broadcasted_iota(jnp.int32, (2 * c_, c_), 1)
    bound = jnp.where(row < c_, row + 1, ((row - c_) // r_) * r_)
    lmat = jnp.where(col < bound, 1.0, 0.0).astype(BF16)
    ur = lax.broadcasted_iota(jnp.int32, (c_, c_), 0)
    uc = lax.broadcasted_iota(jnp.int32, (c_, c_), 1)
    umat = jnp.where(ur <= uc, 1.0, 0.0).astype(BF16)
    lane = lax.broadcasted_iota(jnp.int32, (1, HK), 1)
    hmask = [(lane >= hh * DK_A) & (lane < (hh + 1) * DK_A) for hh in range(H_A)]
    g_out = g_ref[...]

    def stack_heads(a):
        return jnp.concatenate([jnp.where(hmask[hh], a, 0.0) for hh in range(H_A)], axis=0).astype(BF16)

    for c in range(tc // c_):
        sl = slice(c * c_, (c + 1) * c_)
        g1, g2, g3 = _split3(la_ref[0, sl, :])
        bm = _mm(lmat, g1) + _mm(lmat, g2) + _mm(lmat, g3)
        b, mrow = bm[:c_], bm[c_:]
        t1, t2, t3 = _split3(laT_ref[0, :, sl])
        bT = _mm(t1, umat) + _mm(t2, umat) + _mm(t3, umat)
        q = q_ref[0, sl, :]
        k = k_ref[0, sl, :]
        v = v_ref[0, sl, :].astype(BF16)
        s_old = s_ref[...]

        o_inter = _mm(stack_heads(q * jnp.exp(b)), s_old.astype(BF16))

        qt = q * jnp.exp(b - mrow)
        o_sub = [[] for _ in range(H_A)]
        for i in range(nsub):
            n_k = (i + 1) * r_
            m_i = mrow[i * r_:i * r_ + 1, :]
            kt = (k[:n_k] * jnp.exp(jnp.minimum(m_i - b[:n_k], GLA_EXP_CLAMP))).astype(BF16)
            att = _mm_nt(stack_heads(qt[i * r_:(i + 1) * r_]), kt)
            rr = lax.broadcasted_iota(jnp.int32, (H_A * r_, n_k), 0) % r_
            cc = lax.broadcasted_iota(jnp.int32, (H_A * r_, n_k), 1)
            att = jnp.where(cc <= rr + i * r_, att, 0.0).astype(BF16)
            pv = _mm(att, v[:n_k])
            for hh in range(H_A):
                o_sub[hh].append(pv[hh * r_:(hh + 1) * r_, hh * DV_A:(hh + 1) * DV_A])
        for hh in range(H_A):
            o_h = o_inter[hh * c_:(hh + 1) * c_] + jnp.concatenate(o_sub[hh], axis=0)
            gate = r_ref[0, sl, hh * DV_A:(hh + 1) * DV_A].astype(F32)
            o_ref[0, sl, hh * DV_A:(hh + 1) * DV_A] = (_rms(o_h, g_out) * _silu(gate)).astype(o_ref.dtype)

        bl = bT[:, c_ - 1:c_]
        klT = (kT_ref[0, :, sl] * jnp.exp(bl - bT)).astype(BF16)
        upd = _mm(klT, v)
        decay = jnp.exp(bl)
        s_ref[...] = jnp.concatenate(
            [s_old[hh * DK_A:(hh + 1) * DK_A] * decay[hh * DK_A:(hh + 1) * DK_A]
             + upd[hh * DK_A:(hh + 1) * DK_A, hh * DV_A:(hh + 1) * DV_A] for hh in range(H_A)], axis=0)

    @pl.when(j == pl.num_programs(1) - 1)
    def _():
        sfin_ref[0] = s_ref[...]


def _gla(qa, ka, la, kT, laT, va, ra, s0, g_out, *, chunk, sub, tc, out_dtype):
    nb, t, _ = qa.shape
    grid = (nb, t // tc)
    tok = lambda width: pl.BlockSpec((1, tc, width), lambda b, j: (b, j, 0))
    tsp = pl.BlockSpec((1, HK, tc), lambda b, j: (b, 0, j))
    st = pl.BlockSpec((1, HK, DV_A), lambda b, j: (b, 0, 0))
    return pl.pallas_call(
        functools.partial(_gla_kernel, chunk=chunk, sub=sub),
        out_shape=(jax.ShapeDtypeStruct((nb, t, HV), out_dtype),
                   jax.ShapeDtypeStruct((nb, HK, DV_A), F32)),
        grid=grid,
        in_specs=[tok(HK), tok(HK), tok(HK), tsp, tsp, tok(HV), tok(HV), st, _const_spec((1, DV_A))],
        out_specs=(tok(HV), st),
        scratch_shapes=[pltpu.VMEM((HK, DV_A), F32)],
        compiler_params=_cparams(("arbitrary", "arbitrary")), name="gla",
    )(qa, ka, la, kT, laT, va, ra, s0, g_out)


def _softmax_step(s, v, m_ref, l_ref, acc_ref):
    m_prev = m_ref[:, 0:1]
    m_new = jnp.maximum(m_prev, jnp.max(s, axis=-1, keepdims=True))
    alpha = jnp.exp(m_prev - m_new)
    p = jnp.exp(s - m_new)
    l_new = alpha * l_ref[:, 0:1] + jnp.sum(p, axis=-1, keepdims=True)
    acc_ref[...] = alpha * acc_ref[...] + _mm(p.astype(BF16), v)
    m_ref[...] = jnp.broadcast_to(m_new, m_ref.shape)
    l_ref[...] = jnp.broadcast_to(l_new, l_ref.shape)


def _softmax_init(m_ref, l_ref, acc_ref):
    m_ref[...] = jnp.full(m_ref.shape, -jnp.inf, F32)
    l_ref[...] = jnp.zeros(l_ref.shape, F32)
    acc_ref[...] = jnp.zeros(acc_ref.shape, F32)


def _flash_kernel(qi_ref, kj_ref, q_ref, k_ref, wuv_ref, g_ref, o_ref, m_ref, l_ref, acc_ref, *, scale):
    step = pl.program_id(1)
    qi, kj = qi_ref[step], kj_ref[step]
    tq = q_ref.shape[2]
    tk = k_ref.shape[1]
    rows = H_B * tq

    @pl.when(kj == 0)
    def _():
        _softmax_init(m_ref, l_ref, acc_ref)

    kc = k_ref[0]
    s = _mm_nt(q_ref[0].reshape(rows, QCAT), kc) * scale

    @pl.when(kj < qi)
    def _():
        _softmax_step(s, kc[:, 0:KV_LORA], m_ref, l_ref, acc_ref)

    @pl.when(kj == qi)
    def _():
        t_q = lax.broadcasted_iota(jnp.int32, (rows, tk), 0) % tq
        t_k = lax.broadcasted_iota(jnp.int32, (rows, tk), 1)
        _softmax_step(jnp.where(t_k <= t_q, s, -jnp.inf), kc[:, 0:KV_LORA], m_ref, l_ref, acc_ref)
        o_lat = (acc_ref[...] / l_ref[:, 0:1]).astype(BF16)
        for hh in range(H_B):
            ob = _mm(o_lat[hh * tq:(hh + 1) * tq], wuv_ref[hh])
            o_ref[0, :, hh * V_B:(hh + 1) * V_B] = _rms(ob, g_ref[...]).astype(o_ref.dtype)


def _flash(qcat, kcat, wuv, g_out, *, tq):
    nb, _, t, _ = qcat.shape
    nq = t // tq
    pairs = [(i, j) for i in range(nq) for j in range(i + 1)]
    qi_tab = jnp.asarray([p[0] for p in pairs], jnp.int32)
    kj_tab = jnp.asarray([p[1] for p in pairs], jnp.int32)
    scale = (QK_NOPE + QK_ROPE) ** -0.5
    rows = H_B * tq
    grid_spec = pltpu.PrefetchScalarGridSpec(
        num_scalar_prefetch=2,
        grid=(nb, len(pairs)),
        in_specs=[pl.BlockSpec((1, H_B, tq, QCAT), lambda b, s, qi, kj: (b, 0, qi[s], 0)),
                  pl.BlockSpec((1, tq, QCAT), lambda b, s, qi, kj: (b, kj[s], 0)),
                  pl.BlockSpec((H_B, KV_LORA, V_B), lambda b, s, qi, kj: (0, 0, 0)),
                  pl.BlockSpec((1, V_B), lambda b, s, qi, kj: (0, 0))],
        out_specs=pl.BlockSpec((1, tq, H_B * V_B), lambda b, s, qi, kj: (b, qi[s], 0)),
        scratch_shapes=[pltpu.VMEM((rows, LANE), F32), pltpu.VMEM((rows, LANE), F32),
                        pltpu.VMEM((rows, KV_LORA), F32)],
    )
    return pl.pallas_call(
        functools.partial(_flash_kernel, scale=scale),
        out_shape=jax.ShapeDtypeStruct((nb, t, H_B * V_B), BF16),
        grid_spec=grid_spec,
        compiler_params=_cparams(("arbitrary", "arbitrary")), name="flash",
    )(qi_tab, kj_tab, qcat, kcat, wuv, g_out)


def _decode_kernel(pt_ref, q_ref, knew_ref, wuv_ref, g_ref, *rest, scale, npages, td):
    lat_refs = rest[:npages]
    kr_refs = rest[npages:2 * npages]
    o_ref, m_ref, l_ref, acc_ref = rest[2 * npages:]
    step = pl.program_id(1)
    rows = H_B * td

    @pl.when(step == 0)
    def _():
        _softmax_init(m_ref, l_ref, acc_ref)

    q = q_ref[0].astype(BF16)
    ql, qr = q[:, 0:KV_LORA], q[:, KV_LORA:KV_LORA + QK_ROPE]
    lats = [r[0].astype(BF16) for r in lat_refs]
    s = jnp.concatenate(
        [_mm_nt(ql, lats[p]) + _mm_nt(qr, kr_refs[p][0].astype(BF16)) for p in range(npages)], axis=1) * scale
    _softmax_step(s, jnp.concatenate(lats, axis=0), m_ref, l_ref, acc_ref)

    @pl.when(step == pl.num_programs(1) - 1)
    def _():
        kn = knew_ref[0].astype(BF16)
        s_new = _mm_nt(q, kn) * scale
        t_q = lax.broadcasted_iota(jnp.int32, s_new.shape, 0) % td
        t_k = lax.broadcasted_iota(jnp.int32, s_new.shape, 1)
        _softmax_step(jnp.where(t_k <= t_q, s_new, -jnp.inf), kn[:, 0:KV_LORA], m_ref, l_ref, acc_ref)
        o_lat = (acc_ref[...] / l_ref[:, 0:1]).astype(BF16)
        ob = _mm(o_lat, wuv_ref[...])
        for hh in range(H_B):
            blk = ob[hh * td:(hh + 1) * td, hh * V_B:(hh + 1) * V_B]
            o_ref[0, :, hh * V_B:(hh + 1) * V_B] = _rms(blk, g_ref[...]).astype(o_ref.dtype)


def _decode(qcat, knew, wuv_all, g_out, cache_lat, cache_kr, page_table, *, npages):
    nb, _, td, _ = qcat.shape
    qcat = qcat.reshape(nb, H_B * td, QCAT)
    n_pages = page_table.shape[1]
    page = cache_lat.shape[1]
    nsteps = n_pages // npages
    scale = (QK_NOPE + QK_ROPE) ** -0.5
    rows = H_B * td
    pt_flat = page_table.reshape(-1)

    def page_spec(width, p):
        return pl.BlockSpec((1, page, width),
                            lambda b, s, pt: (pt[b * n_pages + s * npages + p], 0, 0))

    grid_spec = pltpu.PrefetchScalarGridSpec(
        num_scalar_prefetch=1,
        grid=(nb, nsteps),
        in_specs=[pl.BlockSpec((1, rows, QCAT), lambda b, s, pt: (b, 0, 0)),
                  pl.BlockSpec((1, LANE, QCAT), lambda b, s, pt: (b, 0, 0)),
                  pl.BlockSpec(wuv_all.shape, lambda b, s, pt: (0, 0)),
                  pl.BlockSpec((1, V_B), lambda b, s, pt: (0, 0))]
                 + [page_spec(KV_LORA, p) for p in range(npages)]
                 + [page_spec(QK_ROPE, p) for p in range(npages)],
        out_specs=pl.BlockSpec((1, td, H_B * V_B), lambda b, s, pt: (b, 0, 0)),
        scratch_shapes=[pltpu.VMEM((rows, LANE), F32), pltpu.VMEM((rows, LANE), F32),
                        pltpu.VMEM((rows, KV_LORA), F32)],
    )
    return pl.pallas_call(
        functools.partial(_decode_kernel, scale=scale, npages=npages, td=td),
        out_shape=jax.ShapeDtypeStruct((nb, td, H_B * V_B), F32),
        grid_spec=grid_spec,
        compiler_params=_cparams(("arbitrary", "arbitrary")), name="decode",
    )(pt_flat, qcat, knew, wuv_all, g_out, *([cache_lat] * npages), *([cache_kr] * npages))


def _outproj_kernel(oa_ref, ob_ref, x_ref, mod_ref, wo_ref, gpost_ref, gpre_ref, x1_ref, h2_ref):
    bb, tt, d = x_ref.shape
    m = bb * tt
    mod = mod_ref[...]
    gt1, sh2, sc2 = mod[:, :, 2 * d:3 * d], mod[:, :, 3 * d:4 * d], mod[:, :, 4 * d:5 * d]
    mix = (_mm(oa_ref[...].reshape(m, HV).astype(BF16), wo_ref[0:HV, :])
           + _mm(ob_ref[...].reshape(m, H_B * V_B).astype(BF16), wo_ref[HV:HV + H_B * V_B, :]))
    x1 = x_ref[...] + gt1 * _rms(mix, gpost_ref[...]).reshape(bb, tt, d)
    x1_ref[...] = x1
    h2_ref[...] = (_rms(x1, gpre_ref[...]) * (1.0 + sc2) + sh2).astype(h2_ref.dtype)


def _outproj(oa, ob, x, mod3, w, *, bb, tt, act_dtype):
    nb, t, d = x.shape
    grid = (nb // bb, t // tt)
    tok = lambda width: pl.BlockSpec((bb, tt, width), lambda i, j: (i, j, 0))
    return pl.pallas_call(
        _outproj_kernel,
        out_shape=(jax.ShapeDtypeStruct((nb, t, d), F32), jax.ShapeDtypeStruct((nb, t, d), act_dtype)),
        grid=grid,
        in_specs=[tok(HV), tok(H_B * V_B), tok(d),
                  pl.BlockSpec((bb, 1, mod3.shape[2]), lambda i, j: (i, 0, 0)),
                  _const_spec(w["wo"].shape), _const_spec((1, d)), _const_spec((1, d))],
        out_specs=(tok(d), tok(d)),
        compiler_params=_cparams(("arbitrary", "arbitrary")), name="outproj",
    )(oa, ob, x, mod3, w["wo"], w["g_post_mix"], w["g_pre_ffn"])


def _gelu_tanh(x):
    return 0.5 * x * (1.0 + jnp.tanh(math.sqrt(2.0 / math.pi) * (x + 0.044715 * (x * x * x))))


def _ffn_kernel(h2_ref, x1_ref, mod_ref, cst_ref, wup_ref, wc_ref, bc_ref, wdn_ref, gpost_ref,
                y_ref, cnew_ref, u_ref, *, fchunk):
    bb, tt, d = x1_ref.shape
    m = bb * tt
    ff = wdn_ref.shape[0]
    j = pl.program_id(1)
    pad = SUBLANE

    @pl.when(j == 0)
    def _():
        u_ref[:, pad - (CONV_W - 1):pad, :] = cst_ref[...]

    h2 = h2_ref[...].reshape(m, d).astype(BF16)
    acc = jnp.zeros((m, d), F32)
    for c in range(ff // fchunk):
        halves = []
        for off in (c * fchunk, ff + c * fchunk):
            cols = slice(off, off + fchunk)
            u_ref[:, pad:pad + tt, cols] = _mm(h2, wup_ref[:, cols]).reshape(bb, tt, fchunk)
            uc = bc_ref[:, cols][None]
            for tap in range(CONV_W):
                lo = pad - (CONV_W - 1) + tap
                uc = uc + wc_ref[tap:tap + 1, cols][None] * u_ref[:, lo:lo + tt, cols]
            halves.append(uc.reshape(m, fchunk))
        act = (halves[0] * _gelu_tanh(halves[1])).astype(BF16)
        acc = acc + _mm(act, wdn_ref[c * fchunk:(c + 1) * fchunk, :])
    gt2 = mod_ref[...][:, :, 5 * d:6 * d]
    y_ref[...] = x1_ref[...] + gt2 * _rms(acc, gpost_ref[...]).reshape(bb, tt, d)
    tail = u_ref[:, pad + tt - (CONV_W - 1):pad + tt, :]
    cnew_ref[...] = tail
    u_ref[:, pad - (CONV_W - 1):pad, :] = tail


def _ffn(h2, x1, mod3, cstate, w, *, bb, tt, fchunk):
    nb, t, d = x1.shape
    f2 = w["wup"].shape[1]
    grid = (nb // bb, t // tt)
    tok = lambda width: pl.BlockSpec((bb, tt, width), lambda i, j: (i, j, 0))
    cs_spec = pl.BlockSpec((bb, CONV_W - 1, f2), lambda i, j: (i, 0, 0))
    return pl.pallas_call(
        functools.partial(_ffn_kernel, fchunk=fchunk),
        out_shape=(jax.ShapeDtypeStruct((nb, t, d), F32),
                   jax.ShapeDtypeStruct((nb, CONV_W - 1, f2), F32)),
        grid=grid,
        in_specs=[tok(d), tok(d),
                  pl.BlockSpec((bb, 1, mod3.shape[2]), lambda i, j: (i, 0, 0)),
                  cs_spec,
                  _const_spec(w["wup"].shape), _const_spec((CONV_W, f2)), _const_spec((1, f2)),
                  _const_spec(w["wdn"].shape), _const_spec((1, d))],
        out_specs=(tok(d), cs_spec),
        scratch_shapes=[pltpu.VMEM((bb, SUBLANE + tt, f2), F32)],
        compiler_params=_cparams(("arbitrary", "arbitrary")), name="ffn",
    )(h2, x1, mod3, cstate, w["wup"], w["wconv"], w["bconv"], w["wdn"], w["g_post_ffn"])


def _rot_cols(wr):
    half = wr.shape[-1] // 2
    return jnp.concatenate([-wr[..., half:], wr[..., :half]], axis=-1)


def _prep_weights(l, w_in, w_gla_a2, b_gla_a, g_gla_out, g_mla_q, g_mla_kv, w_mla_uq, w_mla_uk, w_mla_uv,
                  g_mla_out, w_o, w_ffn_up, w_ffn_conv, b_ffn_conv, w_ffn_down, g_pre_mix, g_post_mix,
                  g_pre_ffn, g_post_ffn):
    wi = w_in[l]
    d = wi.shape[0]
    o_gr = 2 * HK + 2 * HV
    o_cq = o_gr + GATE_RANK
    o_ckv = o_cq + Q_LORA
    o_kr = o_ckv + KV_LORA
    kr = wi[:, o_kr:o_kr + QK_ROPE]
    win = jnp.concatenate(
        [wi[:, 0:o_gr], wi[:, o_cq:o_ckv], wi[:, o_ckv:o_kr], kr, _rot_cols(kr),
         wi[:, o_gr:o_cq], jnp.zeros((d, LANE - GATE_RANK), wi.dtype)], axis=1).astype(BF16)
    wa2 = jnp.concatenate([w_gla_a2[l], jnp.zeros((LANE - GATE_RANK, HK), F32)], axis=0).astype(BF16)
    uq = w_mla_uq[l].reshape(Q_LORA, H_B, QK_NOPE + QK_ROPE)
    uq_rope = uq[:, :, QK_NOPE:]
    wuq = jnp.concatenate(
        [uq[:, :, :QK_NOPE].reshape(Q_LORA, H_B * QK_NOPE),
         jnp.concatenate([uq_rope, _rot_cols(uq_rope)], axis=-1).reshape(Q_LORA, H_B * 2 * QK_ROPE)],
        axis=1).astype(BF16)
    return dict(
        win=win, wkT=wi[:, HK:2 * HK].T.astype(BF16),
        wa2=wa2, ba=b_gla_a[l].reshape(1, HK), wa2T=wa2.T, baT=b_gla_a[l].reshape(HK, 1),
        g_gla_out=g_gla_out[l].reshape(1, DV_A),
        g_mla_q=g_mla_q[l].reshape(1, Q_LORA), g_mla_kv=g_mla_kv[l].reshape(1, KV_LORA),
        wuq=wuq, wukT=jnp.transpose(w_mla_uk[l], (1, 2, 0)).astype(BF16),
        wuv=jnp.transpose(w_mla_uv[l], (1, 0, 2)).astype(BF16),
        wuv_all=w_mla_uv[l].reshape(KV_LORA, H_B * V_B).astype(BF16),
        g_mla_out=g_mla_out[l].reshape(1, V_B),
        wo=w_o[l].astype(BF16), wup=w_ffn_up[l].astype(BF16), wconv=w_ffn_conv[l],
        bconv=b_ffn_conv[l].reshape(1, -1), wdn=w_ffn_down[l].astype(BF16),
        g_pre_mix=g_pre_mix[l].reshape(1, d), g_post_mix=g_post_mix[l].reshape(1, d),
        g_pre_ffn=g_pre_ffn[l].reshape(1, d), g_post_ffn=g_post_ffn[l].reshape(1, d),
    )


def _rope_table(pos):
    inv = ROPE_BASE ** (-jnp.arange(0, QK_ROPE, 2, dtype=F32) / QK_ROPE)
    ang = pos.astype(F32)[:, None] * inv[None, :]
    cos, sin = jnp.cos(ang), jnp.sin(ang)
    return jnp.concatenate([cos, cos, sin, sin], axis=-1)


def _token_tile(t, target):
    tt = min(t, target)
    while t % tt:
        tt //= 2
    return tt


def _layer(x, mod3, cs, w, gla_s0, conv_state, *, per_seq, attend):
    nb, t, d = x.shape
    if per_seq:
        bb, tt = nb, t
    else:
        bb, tt = 1, _token_tile(t, 512)
    act_dtype = F32 if per_seq else BF16
    qa, ka, va, ra, la, kT, laT, qcat, kcat, lat, kr = _inproj(x, mod3, cs, w, bb=bb, tt=tt, act_dtype=act_dtype)

    if per_seq:
        tp = 2 * SUBLANE
        padt = lambda a: jnp.pad(a, ((0, 0), (0, tp - t), (0, 0)))
        tr = lambda a: jnp.pad(a.reshape(HK, nb, t).transpose(1, 0, 2), ((0, 0), (0, 0), (0, tp - t)))
        o_a, s_fin = _gla(padt(qa), padt(ka), padt(la), tr(kT), tr(laT), padt(va), padt(ra), gla_s0,
                          w["g_gla_out"], chunk=tp, sub=tp, tc=tp, out_dtype=act_dtype)
        o_a = o_a[:, :t]
    else:
        o_a, s_fin = _gla(qa, ka, la, kT, laT, va, ra, gla_s0, w["g_gla_out"],
                          chunk=LANE, sub=LANE // 4, tc=_token_tile(t, 256), out_dtype=act_dtype)

    o_b = attend(qcat, kcat)
    x1, h2 = _outproj(o_a, o_b, x, mod3, w, bb=bb, tt=tt, act_dtype=act_dtype)
    ffn_tt = tt if per_seq else _token_tile(t, 256)
    y, c_new = _ffn(h2, x1, mod3, conv_state, w, bb=bb, tt=ffn_tt, fchunk=w["wdn"].shape[0] // 2)
    return y, lat, kr, s_fin, c_new


def kernel(x_prompt, x_sample, c_prompt, c_sample, cache_latent, cache_krope, state_gla, state_conv, page_table, w_ada, b_ada, g_pre_mix, g_post_mix, g_pre_ffn, g_post_ffn, w_in, w_gla_a2, b_gla_a, g_gla_out, g_mla_q, g_mla_kv, w_mla_uq, w_mla_uk, w_mla_uv, g_mla_out, w_o, w_ffn_up, w_ffn_conv, b_ffn_conv, w_ffn_down):
    depth = w_ada.shape[0]
    nbp, tp, d = x_prompt.shape
    nbs, ts, _ = x_sample.shape
    n_pages, page = page_table.shape[1], cache_latent.shape[2]
    past_len = n_pages * page
    f2 = w_ffn_up.shape[2]
    cs_p = _rope_table(jnp.arange(tp, dtype=jnp.int32))
    cs_s = _rope_table(past_len + jnp.arange(ts, dtype=jnp.int32))

    hp, hs = x_prompt, x_sample
    outs = [[] for _ in range(8)]
    for l in range(depth):
        w = _prep_weights(l, w_in, w_gla_a2, b_gla_a, g_gla_out, g_mla_q, g_mla_kv, w_mla_uq, w_mla_uk,
                          w_mla_uv, g_mla_out, w_o, w_ffn_up, w_ffn_conv, b_ffn_conv, w_ffn_down,
                          g_pre_mix, g_post_mix, g_pre_ffn, g_post_ffn)
        mod = _ada(jnp.concatenate([c_prompt, c_sample], axis=0), w_ada[l], b_ada[l])
        mod3 = mod.reshape(nbp + nbs, 1, mod.shape[1])

        def attend_prompt(qcat, kcat):
            return _flash(qcat, kcat, w["wuv"], w["g_mla_out"], tq=_token_tile(tp, 256))

        def attend_sample(qcat, kcat):
            knew = jnp.pad(kcat, ((0, 0), (0, LANE - ts), (0, 0)))
            return _decode(qcat, knew, w["wuv_all"], w["g_mla_out"], cache_latent[l], cache_krope[l],
                           page_table, npages=min(16, n_pages))

        hp, a1, a2, a3, a4 = _layer(
            hp, mod3[:nbp], cs_p, w, jnp.zeros((nbp, HK, DV_A), F32),
            jnp.zeros((nbp, CONV_W - 1, f2), F32), per_seq=False, attend=attend_prompt)
        hs, b1, b2, b3, b4 = _layer(
            hs, mod3[nbp:], cs_s, w, state_gla[l].reshape(nbs, HK, DV_A), state_conv[l],
            per_seq=True, attend=attend_sample)
        for lst, val in zip(outs, (a1, a2, a3.reshape(nbp, H_A, DK_A, DV_A), a4,
                                   b1, b2, b3.reshape(nbs, H_A, DK_A, DV_A), b4)):
            lst.append(val)
    return (hp, hs) + tuple(jnp.stack(o) for o in outs)
```

```python
import functools
import math

import jax
import jax.numpy as jnp
import numpy as np
from jax import lax
from jax.experimental import pallas as pl
from jax.experimental.pallas import tpu as pltpu

F32 = jnp.float32
BF16 = jnp.bfloat16

EPS = 1e-6
H_A, DK_A, DV_A = 4, 64, 128
GATE_RANK = 16
GATE_TAU = 16.0
H_B = 4
Q_LORA, KV_LORA = 384, 256
QK_NOPE, QK_ROPE, V_B = 128, 64, 128
ROPE_BASE = 10000.0
CONV_W = 3

HK = H_A * DK_A
HV = H_A * DV_A
QCAT = KV_LORA + 2 * QK_ROPE
LANE = 128
SUBLANE = 8

_O_QA, _O_KA, _O_VA, _O_RA = 0, HK, 2 * HK, 2 * HK + HV
_O_CQ = 2 * HK + 2 * HV
_O_CKV = _O_CQ + Q_LORA
_O_KRR = _O_CKV + KV_LORA
_O_GR = _O_KRR + 2 * QK_ROPE
IN_EXT = _O_GR + LANE

QK_SCALE_LOG2E = (QK_NOPE + QK_ROPE) ** -0.5 * math.log2(math.e)

VMEM_LIMIT = 56 * 1024 * 1024
GLA_EXP_CLAMP = 80.0


def _mm(a, b):
    return jnp.dot(a, b, preferred_element_type=F32)


def _mm_nt(a, b):
    return lax.dot_general(a, b, (((1,), (1,)), ((), ())), preferred_element_type=F32)


def _rms(x, g):
    return x * lax.rsqrt(jnp.mean(x * x, axis=-1, keepdims=True) + EPS) * g


def _silu(x):
    return x / (1.0 + jnp.exp(-x))


def _log_sigmoid(x):
    return -(jnp.maximum(-x, 0.0) + jnp.log(1.0 + jnp.exp(-jnp.abs(x))))


def _split3(x):
    x1 = x.astype(BF16)
    r1 = x - x1.astype(F32)
    x2 = r1.astype(BF16)
    x3 = (r1 - x2.astype(F32)).astype(BF16)
    return x1, x2, x3


def _cparams(sem):
    return pltpu.CompilerParams(dimension_semantics=sem, vmem_limit_bytes=VMEM_LIMIT)


def _const_spec(shape):
    nd = len(shape)
    return pl.BlockSpec(shape, lambda *_: (0,) * nd)


def _ada_kernel(c_ref, w_ref, b_ref, o_ref):
    s = _silu(c_ref[...]).astype(BF16)
    o_ref[...] = _mm(s, w_ref[...].astype(BF16)) + b_ref[...]


def _ada(c_all, w_ada, b_ada):
    n, d = c_all.shape
    nout = w_ada.shape[1]
    tn = d
    return pl.pallas_call(
        _ada_kernel,
        out_shape=jax.ShapeDtypeStruct((n, nout), F32),
        grid=(nout // tn,),
        in_specs=[pl.BlockSpec((n, d), lambda j: (0, 0)),
                  pl.BlockSpec((d, tn), lambda j: (0, j)),
                  pl.BlockSpec((1, tn), lambda j: (0, j))],
        out_specs=pl.BlockSpec((n, tn), lambda j: (0, j)),
        compiler_params=_cparams(("arbitrary",)),
        name="ada",
    )(c_all, w_ada, b_ada.reshape(1, nout))


def _rope_sum(a3, cs):
    bb, tt, w = a3.shape
    a = (a3 * cs[None]).reshape(bb * tt, w)
    return a + pltpu.roll(a, QK_ROPE, axis=1)


def _inproj_kernel(x_ref, mod_ref, gpre_ref, win_ref, wkT_ref, wa2_ref, ba_ref, wa2T_ref, baT_ref,
                   gq_ref, gkv_ref, wuq_ref, wukT_ref, cs_ref,
                   qa_ref, ka_ref, va_ref, ra_ref, la_ref, kT_ref, laT_ref,
                   qcat_ref, kcat_ref, lat_ref, kr_ref):
    bb, tt, d = x_ref.shape
    m = bb * tt
    x = x_ref[...]
    mod = mod_ref[...]
    sh1, sc1 = mod[:, :, 0:d], mod[:, :, d:2 * d]
    h = _rms(x, gpre_ref[...]) * (1.0 + sc1) + sh1
    h = h.reshape(m, d).astype(BF16)
    z = _mm(h, win_ref[...])

    qa_ref[...] = (z[:, _O_QA:_O_QA + HK] * (DK_A ** -0.5)).reshape(bb, tt, HK)
    ka_ref[...] = z[:, _O_KA:_O_KA + HK].reshape(bb, tt, HK)
    va_ref[...] = z[:, _O_VA:_O_VA + HV].astype(va_ref.dtype).reshape(bb, tt, HV)
    ra_ref[...] = z[:, _O_RA:_O_RA + HV].astype(ra_ref.dtype).reshape(bb, tt, HV)
    gr = z[:, _O_GR:_O_GR + LANE].astype(BF16)
    la = _log_sigmoid(_mm(gr, wa2_ref[...]) + ba_ref[...]) * (1.0 / GATE_TAU)
    la_ref[...] = la.reshape(bb, tt, HK)
    kT_ref[...] = _mm_nt(wkT_ref[...], h)
    laT_ref[...] = _log_sigmoid(_mm_nt(wa2T_ref[...], gr) + baT_ref[...]) * (1.0 / GATE_TAU)

    cs = cs_ref[...]
    ckv = _rms(z[:, _O_CKV:_O_CKV + KV_LORA], gkv_ref[...])
    lat_ref[...] = ckv.reshape(bb, tt, KV_LORA)
    krs = _rope_sum(z[:, _O_KRR:_O_KRR + 2 * QK_ROPE].reshape(bb, tt, 2 * QK_ROPE), cs)
    kr_ref[...] = krs[:, 0:QK_ROPE].reshape(bb, tt, QK_ROPE)
    lane = lax.broadcasted_iota(jnp.int32, (m, 2 * QK_ROPE), 1)
    kcat_ref[:, :, 0:KV_LORA] = ckv.astype(kcat_ref.dtype).reshape(bb, tt, KV_LORA)
    kcat_ref[:, :, KV_LORA:QCAT] = (
        jnp.where(lane < QK_ROPE, krs, 0.0).astype(kcat_ref.dtype).reshape(bb, tt, 2 * QK_ROPE))

    cq = _rms(z[:, _O_CQ:_O_CQ + Q_LORA], gq_ref[...]).astype(BF16)
    qb = _mm(cq, wuq_ref[...])
    for hh in range(H_B):
        qn = qb[:, hh * QK_NOPE:(hh + 1) * QK_NOPE].astype(BF16)
        qlat = _mm(qn, wukT_ref[hh])
        qcat_ref[:, hh, :, 0:KV_LORA] = (qlat * QK_SCALE_LOG2E).astype(qcat_ref.dtype).reshape(bb, tt, KV_LORA)
        o = H_B * QK_NOPE + hh * 2 * QK_ROPE
        qrs = _rope_sum(qb[:, o:o + 2 * QK_ROPE].reshape(bb, tt, 2 * QK_ROPE), cs) * QK_SCALE_LOG2E
        qcat_ref[:, hh, :, KV_LORA:QCAT] = qrs.astype(qcat_ref.dtype).reshape(bb, tt, 2 * QK_ROPE)


def _inproj(x, mod3, cs, w, *, bb, tt, act_dtype):
    nb, t, d = x.shape
    grid = (nb // bb, t // tt)
    if bb == 1:
        t_shape = jax.ShapeDtypeStruct((nb, HK, t), F32)
        tsp = pl.BlockSpec((None, HK, tt), lambda i, j: (i, 0, j))
    else:
        t_shape = jax.ShapeDtypeStruct((HK, nb * t), F32)
        tsp = pl.BlockSpec((HK, nb * t), lambda i, j: (0, 0))
    tok = lambda width: pl.BlockSpec((bb, tt, width), lambda i, j: (i, j, 0))
    out_shape = (
        jax.ShapeDtypeStruct((nb, t, HK), F32),
        jax.ShapeDtypeStruct((nb, t, HK), F32),
        jax.ShapeDtypeStruct((nb, t, HV), act_dtype),
        jax.ShapeDtypeStruct((nb, t, HV), act_dtype),
        jax.ShapeDtypeStruct((nb, t, HK), F32),
        t_shape,
        t_shape,
        jax.ShapeDtypeStruct((nb, H_B, t, QCAT), act_dtype),
        jax.ShapeDtypeStruct((nb, t, QCAT), act_dtype),
        jax.ShapeDtypeStruct((nb, t, KV_LORA), F32),
        jax.ShapeDtypeStruct((nb, t, QK_ROPE), F32),
    )
    out_specs = (tok(HK), tok(HK), tok(HV), tok(HV), tok(HK), tsp, tsp,
                 pl.BlockSpec((bb, H_B, tt, QCAT), lambda i, j: (i, 0, j, 0)),
                 tok(QCAT), tok(KV_LORA), tok(QK_ROPE))
    in_specs = [
        tok(d),
        pl.BlockSpec((bb, 1, mod3.shape[2]), lambda i, j: (i, 0, 0)),
        _const_spec((1, d)),
        _const_spec(w["win"].shape), _const_spec(w["wkT"].shape),
        _const_spec(w["wa2"].shape), _const_spec((1, HK)),
        _const_spec(w["wa2T"].shape), _const_spec((HK, 1)),
        _const_spec((1, Q_LORA)), _const_spec((1, KV_LORA)),
        _const_spec(w["wuq"].shape), _const_spec(w["wukT"].shape),
        pl.BlockSpec((tt, 2 * QK_ROPE), lambda i, j: (j, 0)),
    ]
    return pl.pallas_call(
        _inproj_kernel, out_shape=out_shape, grid=grid, in_specs=in_specs, out_specs=out_specs,
        compiler_params=_cparams(("arbitrary", "arbitrary")), name="inproj",
    )(x, mod3, w["g_pre_mix"], w["win"], w["wkT"], w["wa2"], w["ba"], w["wa2T"], w["baT"],
      w["g_mla_q"], w["g_mla_kv"], w["wuq"], w["wukT"], cs)


def _gla_kernel(q_ref, k_ref, la_ref, kT_ref, laT_ref, v_ref, r_ref, s0_ref, g_ref,
                o_ref, sfin_ref, s_ref, *, chunk, sub):
    j = pl.program_id(1)
    c_, r_ = chunk, sub
    nsub = c_ // r_
    tc = q_ref.shape[1]

    @pl.when(j == 0)
    def _():
        s_ref[...] = s0_ref[0]

    row = lax.broadcasted_iota(jnp.int32, (2 * c_, c_), 0)
    col = lax.broadcasted_iota(jnp.int32, (2 * c_, c_), 1)
    bound = jnp.where(row < c_, row + 1, ((row - c_) // r_) * r_)
    lmat = jnp.where(col < bound, 1.0, 0.0).astype(BF16)
    ur = lax.broadcasted_iota(jnp.int32, (c_, c_), 0)
    uc = lax.broadcasted_iota(jnp.int32, (c_, c_), 1)
    umat = jnp.where(ur <= uc, 1.0, 0.0).astype(BF16)
    lane = lax.broadcasted_iota(jnp.int32, (1, HK), 1)
    hmask = [(lane >= hh * DK_A) & (lane < (hh + 1) * DK_A) for hh in range(H_A)]
    g_out = g_ref[...]

    def stack_heads(a):
        return jnp.concatenate([jnp.where(hmask[hh], a, 0.0) for hh in range(H_A)], axis=0).astype(BF16)

    for c in range(tc // c_):
        sl = slice(c * c_, (c + 1) * c_)
        g1, g2, g3 = _split3(la_ref[0, sl, :])
        bm = _mm(lmat, g1) + _mm(lmat, g2) + _mm(lmat, g3)
        b, mrow = bm[:c_], bm[c_:]
        t1, t2, t3 = _split3(laT_ref[0, :, sl])
        bT = _mm(t1, umat) + _mm(t2, umat) + _mm(t3, umat)
        q = q_ref[0, sl, :]
        k = k_ref[0, sl, :]
        v = v_ref[0, sl, :].astype(BF16)
        s_old = s_ref[...]

        o_inter = _mm(stack_heads(q * jnp.exp(b)), s_old.astype(BF16))

        qt = q * jnp.exp(b - mrow)
        o_sub = [[] for _ in range(H_A)]
        for i in range(nsub):
            n_k = (i + 1) * r_
            m_i = mrow[i * r_:i * r_ + 1, :]
            kt = (k[:n_k] * jnp.exp(jnp.minimum(m_i - b[:n_k], GLA_EXP_CLAMP))).astype(BF16)
            att = _mm_nt(stack_heads(qt[i * r_:(i + 1) * r_]), kt)
            rr = lax.broadcasted_iota(jnp.int32, (H_A * r_, n_k), 0) % r_
            cc = lax.broadcasted_iota(jnp.int32, (H_A * r_, n_k), 1)
            att = jnp.where(cc <= rr + i * r_, att, 0.0).astype(BF16)
            pv = _mm(att, v[:n_k])
            for hh in range(H_A):
                o_sub[hh].append(pv[hh * r_:(hh + 1) * r_, hh * DV_A:(hh + 1) * DV_A])
        for hh in range(H_A):
            o_h = o_inter[hh * c_:(hh + 1) * c_] + jnp.concatenate(o_sub[hh], axis=0)
            gate = r_ref[0, sl, hh * DV_A:(hh + 1) * DV_A].astype(F32)
            o_ref[0, sl, hh * DV_A:(hh + 1) * DV_A] = (_rms(o_h, g_out) * _silu(gate)).astype(o_ref.dtype)

        bl = bT[:, c_ - 1:c_]
        klT = (kT_ref[0, :, sl] * jnp.exp(bl - bT)).astype(BF16)
        upd = _mm(klT, v)
        decay = jnp.exp(bl)
        s_ref[...] = jnp.concatenate(
            [s_old[hh * DK_A:(hh + 1) * DK_A] * decay[hh * DK_A:(hh + 1) * DK_A]
             + upd[hh * DK_A:(hh + 1) * DK_A, hh * DV_A:(hh + 1) * DV_A] for hh in range(H_A)], axis=0)

    @pl.when(j == pl.num_programs(1) - 1)
    def _():
        sfin_ref[0] = s_ref[...]


def _gla(qa, ka, la, kT, laT, va, ra, s0, g_out, *, chunk, sub, tc, out_dtype):
    nb, t, _ = qa.shape
    grid = (nb, t // tc)
    tok = lambda width: pl.BlockSpec((1, tc, width), lambda b, j: (b, j, 0))
    tsp = pl.BlockSpec((1, HK, tc), lambda b, j: (b, 0, j))
    st = pl.BlockSpec((1, HK, DV_A), lambda b, j: (b, 0, 0))
    return pl.pallas_call(
        functools.partial(_gla_kernel, chunk=chunk, sub=sub),
        out_shape=(jax.ShapeDtypeStruct((nb, t, HV), out_dtype),
                   jax.ShapeDtypeStruct((nb, HK, DV_A), F32)),
        grid=grid,
        in_specs=[tok(HK), tok(HK), tok(HK), tsp, tsp, tok(HV), tok(HV), st, _const_spec((1, DV_A))],
        out_specs=(tok(HV), st),
        scratch_shapes=[pltpu.VMEM((HK, DV_A), F32)],
        compiler_params=_cparams(("arbitrary", "arbitrary")), name="gla",
    )(qa, ka, la, kT, laT, va, ra, s0, g_out)


def _softmax_step(s, v, m_ref, l_ref, acc_ref):
    m_prev = m_ref[:, 0:1]
    m_new = jnp.maximum(m_prev, jnp.max(s, axis=-1, keepdims=True))
    alpha = jnp.exp2(m_prev - m_new)
    p = jnp.exp2(s - m_new)
    l_new = alpha * l_ref[:, 0:1] + jnp.sum(p, axis=-1, keepdims=True)
    acc_ref[...] = alpha * acc_ref[...] + _mm(p.astype(BF16), v)
    m_ref[...] = jnp.broadcast_to(m_new, m_ref.shape)
    l_ref[...] = jnp.broadcast_to(l_new, l_ref.shape)


def _flash_kernel(q_ref, k_ref, wuv_ref, g_ref, o_ref, m_ref, l_ref, acc_ref):
    qi = pl.program_id(1)
    tq = q_ref.shape[2]
    rows = H_B * tq
    q = q_ref[0].reshape(rows, QCAT)
    m_ref[...] = jnp.full(m_ref.shape, -jnp.inf, F32)
    l_ref[...] = jnp.zeros(l_ref.shape, F32)
    acc_ref[...] = jnp.zeros(acc_ref.shape, F32)

    def key_block(j):
        return k_ref[0, pl.ds(pl.multiple_of(j * tq, tq), tq), :]

    def full_block(j, carry):
        kc = key_block(j)
        _softmax_step(_mm_nt(q, kc), kc[:, 0:KV_LORA], m_ref, l_ref, acc_ref)
        return carry

    lax.fori_loop(0, qi, full_block, 0)

    kc = key_block(qi)
    t_q = lax.broadcasted_iota(jnp.int32, (rows, tq), 0) % tq
    t_k = lax.broadcasted_iota(jnp.int32, (rows, tq), 1)
    _softmax_step(jnp.where(t_k <= t_q, _mm_nt(q, kc), -jnp.inf), kc[:, 0:KV_LORA], m_ref, l_ref, acc_ref)
    o_lat = (acc_ref[...] / l_ref[:, 0:1]).astype(BF16)
    for hh in range(H_B):
        ob = _mm(o_lat[hh * tq:(hh + 1) * tq], wuv_ref[hh])
        o_ref[0, :, hh * V_B:(hh + 1) * V_B] = _rms(ob, g_ref[...]).astype(o_ref.dtype)


def _flash(qcat, kcat, wuv, g_out, *, tq):
    nb, _, t, _ = qcat.shape
    rows = H_B * tq
    return pl.pallas_call(
        _flash_kernel,
        out_shape=jax.ShapeDtypeStruct((nb, t, H_B * V_B), BF16),
        grid=(nb, t // tq),
        in_specs=[pl.BlockSpec((1, H_B, tq, QCAT), lambda b, i: (b, 0, i, 0)),
                  pl.BlockSpec((1, t, QCAT), lambda b, i: (b, 0, 0)),
                  _const_spec((H_B, KV_LORA, V_B)), _const_spec((1, V_B))],
        out_specs=pl.BlockSpec((1, tq, H_B * V_B), lambda b, i: (b, i, 0)),
        scratch_shapes=[pltpu.VMEM((rows, LANE), F32), pltpu.VMEM((rows, LANE), F32),
                        pltpu.VMEM((rows, KV_LORA), F32)],
        compiler_params=_cparams(("arbitrary", "arbitrary")), name="flash",
    )(qcat, kcat, wuv, g_out)


def _decode_kernel(pt_ref, q_ref, knew_ref, wuv_ref, g_ref, lat_hbm, krT_hbm, o_ref,
                   lat_buf, kr_buf, s_scr, p_scr, sems, *, n_pages, td, ck):
    b = pl.program_id(0)
    slot = b % 2
    page = lat_hbm.shape[1]
    nck = (n_pages * page) // ck
    ppc = ck // page

    def page_copies(seq, sl, p):
        pg = pt_ref[seq * n_pages + p]
        rows = pl.ds(pl.multiple_of(p * page, page), page)
        return (pltpu.make_async_copy(lat_hbm.at[pg], lat_buf.at[sl, rows, :], sems.at[0, sl]),
                pltpu.make_async_copy(krT_hbm.at[pg], kr_buf.at[sl, p], sems.at[1, sl]))

    def fetch(seq, sl):
        def body(p, carry):
            for cp in page_copies(seq, sl, p):
                cp.start()
            return carry
        lax.fori_loop(0, n_pages, body, 0)

    @pl.when(b == 0)
    def _():
        fetch(0, 0)

    @pl.when(b + 1 < pl.num_programs(0))
    def _():
        fetch(b + 1, 1 - slot)

    def wait_page(p, carry):
        for cp in page_copies(b, slot, p):
            cp.wait()
        return carry
    lax.fori_loop(0, n_pages, wait_page, 0)

    q = q_ref[0].astype(BF16)
    ql, qr = q[:, 0:KV_LORA], q[:, KV_LORA:KV_LORA + QK_ROPE]

    def lat_chunk(c):
        return lat_buf[slot, pl.ds(pl.multiple_of(c * ck, ck), ck), :].astype(BF16)

    def scores(c, carry):
        s_rope = jnp.concatenate(
            [_mm(qr, kr_buf[slot, c * ppc + i].astype(BF16)) for i in range(ppc)], axis=1)
        s_scr[c] = _mm_nt(ql, lat_chunk(c)) + s_rope
        return carry
    lax.fori_loop(0, nck, scores, 0)

    kn = knew_ref[0].astype(BF16)
    s_new = _mm_nt(q, kn)
    t_q = lax.broadcasted_iota(jnp.int32, s_new.shape, 0) % td
    t_k = lax.broadcasted_iota(jnp.int32, s_new.shape, 1)
    s_new = jnp.where(t_k <= t_q, s_new, -jnp.inf)
    s_all = s_scr[...]
    m = jnp.maximum(jnp.max(jnp.max(s_all, axis=0), axis=-1, keepdims=True),
                    jnp.max(s_new, axis=-1, keepdims=True))
    p_all = jnp.exp2(s_all - m[None])
    p_new = jnp.exp2(s_new - m)
    l = jnp.sum(jnp.sum(p_all, axis=0), axis=-1, keepdims=True) + jnp.sum(p_new, axis=-1, keepdims=True)
    p_scr[...] = p_all.astype(BF16)

    def weighted(c, acc):
        return acc + _mm(p_scr[c], lat_chunk(c))
    acc = lax.fori_loop(0, nck, weighted, _mm(p_new.astype(BF16), kn[:, 0:KV_LORA]))
    o_lat = (acc / l).astype(BF16)
    ob = _mm(o_lat, wuv_ref[...])
    for hh in range(H_B):
        blk = ob[hh * td:(hh + 1) * td, hh * V_B:(hh + 1) * V_B]
        o_ref[0, :, hh * V_B:(hh + 1) * V_B] = _rms(blk, g_ref[...]).astype(o_ref.dtype)


def _decode(qcat, knew, wuv_all, g_out, cache_lat, cache_krT, page_table, *, ck):
    nb, _, td, _ = qcat.shape
    qcat = qcat.reshape(nb, H_B * td, QCAT)
    n_pages = page_table.shape[1]
    page = cache_lat.shape[1]
    past = n_pages * page
    rows = H_B * td
    grid_spec = pltpu.PrefetchScalarGridSpec(
        num_scalar_prefetch=1,
        grid=(nb,),
        in_specs=[pl.BlockSpec((1, rows, QCAT), lambda b, pt: (b, 0, 0)),
                  pl.BlockSpec((1, LANE, QCAT), lambda b, pt: (b, 0, 0)),
                  pl.BlockSpec(wuv_all.shape, lambda b, pt: (0, 0)),
                  pl.BlockSpec((1, V_B), lambda b, pt: (0, 0)),
                  pl.BlockSpec(memory_space=pl.ANY),
                  pl.BlockSpec(memory_space=pl.ANY)],
        out_specs=pl.BlockSpec((1, td, H_B * V_B), lambda b, pt: (b, 0, 0)),
        scratch_shapes=[pltpu.VMEM((2, past, KV_LORA), F32),
                        pltpu.VMEM((2, n_pages, QK_ROPE, page), F32),
                        pltpu.VMEM((past // ck, rows, ck), F32),
                        pltpu.VMEM((past // ck, rows, ck), BF16),
                        pltpu.SemaphoreType.DMA((2, 2))],
    )
    return pl.pallas_call(
        functools.partial(_decode_kernel, n_pages=n_pages, td=td, ck=ck),
        out_shape=jax.ShapeDtypeStruct((nb, td, H_B * V_B), F32),
        grid_spec=grid_spec,
        compiler_params=_cparams(("arbitrary",)), name="decode",
    )(page_table.reshape(-1), qcat, knew, wuv_all, g_out, cache_lat, cache_krT)


def _outproj_kernel(oa_ref, ob_ref, x_ref, mod_ref, wo_ref, gpost_ref, gpre_ref, x1_ref, h2_ref):
    bb, tt, d = x_ref.shape
    m = bb * tt
    mod = mod_ref[...]
    gt1, sh2, sc2 = mod[:, :, 2 * d:3 * d], mod[:, :, 3 * d:4 * d], mod[:, :, 4 * d:5 * d]
    mix = (_mm(oa_ref[...].reshape(m, HV).astype(BF16), wo_ref[0:HV, :])
           + _mm(ob_ref[...].reshape(m, H_B * V_B).astype(BF16), wo_ref[HV:HV + H_B * V_B, :]))
    x1 = x_ref[...] + gt1 * _rms(mix, gpost_ref[...]).reshape(bb, tt, d)
    x1_ref[...] = x1
    h2_ref[...] = (_rms(x1, gpre_ref[...]) * (1.0 + sc2) + sh2).astype(h2_ref.dtype)


def _outproj(oa, ob, x, mod3, w, *, bb, tt, act_dtype):
    nb, t, d = x.shape
    grid = (nb // bb, t // tt)
    tok = lambda width: pl.BlockSpec((bb, tt, width), lambda i, j: (i, j, 0))
    return pl.pallas_call(
        _outproj_kernel,
        out_shape=(jax.ShapeDtypeStruct((nb, t, d), F32), jax.ShapeDtypeStruct((nb, t, d), act_dtype)),
        grid=grid,
        in_specs=[tok(HV), tok(H_B * V_B), tok(d),
                  pl.BlockSpec((bb, 1, mod3.shape[2]), lambda i, j: (i, 0, 0)),
                  _const_spec(w["wo"].shape), _const_spec((1, d)), _const_spec((1, d))],
        out_specs=(tok(d), tok(d)),
        compiler_params=_cparams(("arbitrary", "arbitrary")), name="outproj",
    )(oa, ob, x, mod3, w["wo"], w["g_post_mix"], w["g_pre_ffn"])


def _gelu_tanh(x):
    return 0.5 * x * (1.0 + jnp.tanh(math.sqrt(2.0 / math.pi) * (x + 0.044715 * (x * x * x))))


def _ffn_kernel(h2_ref, x1_ref, mod_ref, cst_ref, wup_ref, wc_ref, bc_ref, wdn_ref, gpost_ref,
                y_ref, cnew_ref, u_ref, *, fchunk):
    bb, tt, d = x1_ref.shape
    m = bb * tt
    ff = wdn_ref.shape[0]
    j = pl.program_id(1)
    pad = SUBLANE

    @pl.when(j == 0)
    def _():
        u_ref[:, pad - (CONV_W - 1):pad, :] = cst_ref[...]

    h2 = h2_ref[...].reshape(m, d).astype(BF16)
    acc = jnp.zeros((m, d), F32)
    for c in range(ff // fchunk):
        halves = []
        for off in (c * fchunk, ff + c * fchunk):
            cols = slice(off, off + fchunk)
            u_ref[:, pad:pad + tt, cols] = _mm(h2, wup_ref[:, cols]).reshape(bb, tt, fchunk)
            uc = bc_ref[:, cols][None]
            for tap in range(CONV_W):
                lo = pad - (CONV_W - 1) + tap
                uc = uc + wc_ref[tap:tap + 1, cols][None] * u_ref[:, lo:lo + tt, cols]
            halves.append(uc.reshape(m, fchunk))
        act = (halves[0] * _gelu_tanh(halves[1])).astype(BF16)
        acc = acc + _mm(act, wdn_ref[c * fchunk:(c + 1) * fchunk, :])
    gt2 = mod_ref[...][:, :, 5 * d:6 * d]
    y_ref[...] = x1_ref[...] + gt2 * _rms(acc, gpost_ref[...]).reshape(bb, tt, d)
    tail = u_ref[:, pad + tt - (CONV_W - 1):pad + tt, :]
    cnew_ref[...] = tail
    u_ref[:, pad - (CONV_W - 1):pad, :] = tail


def _ffn(h2, x1, mod3, cstate, w, *, bb, tt, fchunk):
    nb, t, d = x1.shape
    f2 = w["wup"].shape[1]
    grid = (nb // bb, t // tt)
    tok = lambda width: pl.BlockSpec((bb, tt, width), lambda i, j: (i, j, 0))
    cs_spec = pl.BlockSpec((bb, CONV_W - 1, f2), lambda i, j: (i, 0, 0))
    return pl.pallas_call(
        functools.partial(_ffn_kernel, fchunk=fchunk),
        out_shape=(jax.ShapeDtypeStruct((nb, t, d), F32),
                   jax.ShapeDtypeStruct((nb, CONV_W - 1, f2), F32)),
        grid=grid,
        in_specs=[tok(d), tok(d),
                  pl.BlockSpec((bb, 1, mod3.shape[2]), lambda i, j: (i, 0, 0)),
                  cs_spec,
                  _const_spec(w["wup"].shape), _const_spec((CONV_W, f2)), _const_spec((1, f2)),
                  _const_spec(w["wdn"].shape), _const_spec((1, d))],
        out_specs=(tok(d), cs_spec),
        scratch_shapes=[pltpu.VMEM((bb, SUBLANE + tt, f2), F32)],
        compiler_params=_cparams(("arbitrary", "arbitrary")), name="ffn",
    )(h2, x1, mod3, cstate, w["wup"], w["wconv"], w["bconv"], w["wdn"], w["g_post_ffn"])


def _rot_cols(wr):
    half = wr.shape[-1] // 2
    return jnp.concatenate([-wr[..., half:], wr[..., :half]], axis=-1)


def _prep_weights(l, w_in, w_gla_a2, b_gla_a, g_gla_out, g_mla_q, g_mla_kv, w_mla_uq, w_mla_uk, w_mla_uv,
                  g_mla_out, w_o, w_ffn_up, w_ffn_conv, b_ffn_conv, w_ffn_down, g_pre_mix, g_post_mix,
                  g_pre_ffn, g_post_ffn):
    wi = w_in[l]
    d = wi.shape[0]
    o_gr = 2 * HK + 2 * HV
    o_cq = o_gr + GATE_RANK
    o_ckv = o_cq + Q_LORA
    o_kr = o_ckv + KV_LORA
    kr = wi[:, o_kr:o_kr + QK_ROPE]
    win = jnp.concatenate(
        [wi[:, 0:o_gr], wi[:, o_cq:o_ckv], wi[:, o_ckv:o_kr], kr, _rot_cols(kr),
         wi[:, o_gr:o_cq], jnp.zeros((d, LANE - GATE_RANK), wi.dtype)], axis=1).astype(BF16)
    wa2 = jnp.concatenate([w_gla_a2[l], jnp.zeros((LANE - GATE_RANK, HK), F32)], axis=0).astype(BF16)
    uq = w_mla_uq[l].reshape(Q_LORA, H_B, QK_NOPE + QK_ROPE)
    uq_rope = uq[:, :, QK_NOPE:]
    wuq = jnp.concatenate(
        [uq[:, :, :QK_NOPE].reshape(Q_LORA, H_B * QK_NOPE),
         jnp.concatenate([uq_rope, _rot_cols(uq_rope)], axis=-1).reshape(Q_LORA, H_B * 2 * QK_ROPE)],
        axis=1).astype(BF16)
    return dict(
        win=win, wkT=wi[:, HK:2 * HK].T.astype(BF16),
        wa2=wa2, ba=b_gla_a[l].reshape(1, HK), wa2T=wa2.T, baT=b_gla_a[l].reshape(HK, 1),
        g_gla_out=g_gla_out[l].reshape(1, DV_A),
        g_mla_q=g_mla_q[l].reshape(1, Q_LORA), g_mla_kv=g_mla_kv[l].reshape(1, KV_LORA),
        wuq=wuq, wukT=jnp.transpose(w_mla_uk[l], (1, 2, 0)).astype(BF16),
        wuv=jnp.transpose(w_mla_uv[l], (1, 0, 2)).astype(BF16),
        wuv_all=w_mla_uv[l].reshape(KV_LORA, H_B * V_B).astype(BF16),
        g_mla_out=g_mla_out[l].reshape(1, V_B),
        wo=w_o[l].astype(BF16), wup=w_ffn_up[l].astype(BF16), wconv=w_ffn_conv[l],
        bconv=b_ffn_conv[l].reshape(1, -1), wdn=w_ffn_down[l].astype(BF16),
        g_pre_mix=g_pre_mix[l].reshape(1, d), g_post_mix=g_post_mix[l].reshape(1, d),
        g_pre_ffn=g_pre_ffn[l].reshape(1, d), g_post_ffn=g_post_ffn[l].reshape(1, d),
    )


def _rope_table(pos):
    inv = ROPE_BASE ** (-jnp.arange(0, QK_ROPE, 2, dtype=F32) / QK_ROPE)
    ang = pos.astype(F32)[:, None] * inv[None, :]
    cos, sin = jnp.cos(ang), jnp.sin(ang)
    return jnp.concatenate([cos, cos, sin, sin], axis=-1)


def _token_tile(t, target):
    tt = min(t, target)
    while t % tt:
        tt //= 2
    return tt


def _layer(x, mod3, cs, w, gla_s0, conv_state, *, per_seq, attend):
    nb, t, d = x.shape
    if per_seq:
        bb, tt = nb, t
    else:
        bb, tt = 1, _token_tile(t, 512)
    act_dtype = F32 if per_seq else BF16
    qa, ka, va, ra, la, kT, laT, qcat, kcat, lat, kr = _inproj(x, mod3, cs, w, bb=bb, tt=tt, act_dtype=act_dtype)

    if per_seq:
        tp = 2 * SUBLANE
        padt = lambda a: jnp.pad(a, ((0, 0), (0, tp - t), (0, 0)))
        tr = lambda a: jnp.pad(a.reshape(HK, nb, t).transpose(1, 0, 2), ((0, 0), (0, 0), (0, tp - t)))
        o_a, s_fin = _gla(padt(qa), padt(ka), padt(la), tr(kT), tr(laT), padt(va), padt(ra), gla_s0,
                          w["g_gla_out"], chunk=tp, sub=tp, tc=tp, out_dtype=act_dtype)
        o_a = o_a[:, :t]
    else:
        o_a, s_fin = _gla(qa, ka, la, kT, laT, va, ra, gla_s0, w["g_gla_out"],
                          chunk=LANE, sub=LANE // 4, tc=_token_tile(t, 256), out_dtype=act_dtype)

    o_b = attend(qcat, kcat)
    x1, h2 = _outproj(o_a, o_b, x, mod3, w, bb=bb, tt=tt, act_dtype=act_dtype)
    ffn_tt = tt if per_seq else _token_tile(t, 256)
    y, c_new = _ffn(h2, x1, mod3, conv_state, w, bb=bb, tt=ffn_tt, fchunk=w["wdn"].shape[0] // 2)
    return y, lat, kr, s_fin, c_new


def kernel(x_prompt, x_sample, c_prompt, c_sample, cache_latent, cache_krope, state_gla, state_conv, page_table, w_ada, b_ada, g_pre_mix, g_post_mix, g_pre_ffn, g_post_ffn, w_in, w_gla_a2, b_gla_a, g_gla_out, g_mla_q, g_mla_kv, w_mla_uq, w_mla_uk, w_mla_uv, g_mla_out, w_o, w_ffn_up, w_ffn_conv, b_ffn_conv, w_ffn_down):
    depth = w_ada.shape[0]
    nbp, tp, d = x_prompt.shape
    nbs, ts, _ = x_sample.shape
    n_pages, page = page_table.shape[1], cache_latent.shape[2]
    past_len = n_pages * page
    f2 = w_ffn_up.shape[2]
    cs_p = _rope_table(jnp.arange(tp, dtype=jnp.int32))
    cs_s = _rope_table(past_len + jnp.arange(ts, dtype=jnp.int32))

    hp, hs = x_prompt, x_sample
    outs = [[] for _ in range(8)]
    for l in range(depth):
        w = _prep_weights(l, w_in, w_gla_a2, b_gla_a, g_gla_out, g_mla_q, g_mla_kv, w_mla_uq, w_mla_uk,
                          w_mla_uv, g_mla_out, w_o, w_ffn_up, w_ffn_conv, b_ffn_conv, w_ffn_down,
                          g_pre_mix, g_post_mix, g_pre_ffn, g_post_ffn)
        mod = _ada(jnp.concatenate([c_prompt, c_sample], axis=0), w_ada[l], b_ada[l])
        mod3 = mod.reshape(nbp + nbs, 1, mod.shape[1])

        def attend_prompt(qcat, kcat):
            return _flash(qcat, kcat, w["wuv"], w["g_mla_out"], tq=_token_tile(tp, 256))

        def attend_sample(qcat, kcat):
            knew = jnp.pad(kcat, ((0, 0), (0, LANE - ts), (0, 0)))
            return _decode(qcat, knew, w["wuv_all"], w["g_mla_out"], cache_latent[l],
                           jnp.swapaxes(cache_krope[l], 1, 2), page_table, ck=8 * page)

        hp, a1, a2, a3, a4 = _layer(
            hp, mod3[:nbp], cs_p, w, jnp.zeros((nbp, HK, DV_A), F32),
            jnp.zeros((nbp, CONV_W - 1, f2), F32), per_seq=False, attend=attend_prompt)
        hs, b1, b2, b3, b4 = _layer(
            hs, mod3[nbp:], cs_s, w, state_gla[l].reshape(nbs, HK, DV_A), state_conv[l],
            per_seq=True, attend=attend_sample)
        for lst, val in zip(outs, (a1, a2, a3.reshape(nbp, H_A, DK_A, DV_A), a4,
                                   b1, b2, b3.reshape(nbs, H_A, DK_A, DV_A), b4)):
            lst.append(val)
    return (hp, hs) + tuple(jnp.stack(o) for o in outs)
```

```python
import functools
import math

import jax
import jax.numpy as jnp
import numpy as np
from jax import lax
from jax.experimental import pallas as pl
from jax.experimental.pallas import tpu as pltpu

F32 = jnp.float32
BF16 = jnp.bfloat16

EPS = 1e-6
H_A, DK_A, DV_A = 4, 64, 128
GATE_RANK = 16
GATE_TAU = 16.0
H_B = 4
Q_LORA, KV_LORA = 384, 256
QK_NOPE, QK_ROPE, V_B = 128, 64, 128
ROPE_BASE = 10000.0
CONV_W = 3

HK = H_A * DK_A
HV = H_A * DV_A
QCAT = KV_LORA + 2 * QK_ROPE
LANE = 128
SUBLANE = 8

_O_QA, _O_KA, _O_VA, _O_RA = 0, HK, 2 * HK, 2 * HK + HV
_O_CQ = 2 * HK + 2 * HV
_O_CKV = _O_CQ + Q_LORA
_O_KRR = _O_CKV + KV_LORA
_O_GR = _O_KRR + 2 * QK_ROPE
IN_EXT = _O_GR + LANE

QK_SCALE_LOG2E = (QK_NOPE + QK_ROPE) ** -0.5 * math.log2(math.e)

ATTN_TILE = 256
VMEM_LIMIT = 56 * 1024 * 1024
GLA_EXP_CLAMP = 80.0


def _mm(a, b):
    return jnp.dot(a, b, preferred_element_type=F32)


def _mm_nt(a, b):
    return lax.dot_general(a, b, (((1,), (1,)), ((), ())), preferred_element_type=F32)


def _rms(x, g):
    return x * lax.rsqrt(jnp.mean(x * x, axis=-1, keepdims=True) + EPS) * g


def _silu(x):
    return x / (1.0 + jnp.exp(-x))


def _log_sigmoid(x):
    return -(jnp.maximum(-x, 0.0) + jnp.log(1.0 + jnp.exp(-jnp.abs(x))))


def _split3(x):
    x1 = x.astype(BF16)
    r1 = x - x1.astype(F32)
    x2 = r1.astype(BF16)
    x3 = (r1 - x2.astype(F32)).astype(BF16)
    return x1, x2, x3


def _cparams(sem):
    return pltpu.CompilerParams(dimension_semantics=sem, vmem_limit_bytes=VMEM_LIMIT)


def _const_spec(shape):
    nd = len(shape)
    return pl.BlockSpec(shape, lambda *_: (0,) * nd)


def _ada_kernel(c_ref, w_ref, b_ref, o_ref):
    s = _silu(c_ref[...]).astype(BF16)
    o_ref[...] = _mm(s, w_ref[...].astype(BF16)) + b_ref[...]


def _ada(c_all, w_ada, b_ada):
    n, d = c_all.shape
    nout = w_ada.shape[1]
    tn = d
    return pl.pallas_call(
        _ada_kernel,
        out_shape=jax.ShapeDtypeStruct((n, nout), F32),
        grid=(nout // tn,),
        in_specs=[pl.BlockSpec((n, d), lambda j: (0, 0)),
                  pl.BlockSpec((d, tn), lambda j: (0, j)),
                  pl.BlockSpec((1, tn), lambda j: (0, j))],
        out_specs=pl.BlockSpec((n, tn), lambda j: (0, j)),
        compiler_params=_cparams(("arbitrary",)),
        name="ada",
    )(c_all, w_ada, b_ada.reshape(1, nout))


def _rope_sum(a3, cs):
    bb, tt, w = a3.shape
    a = (a3 * cs[None]).reshape(bb * tt, w)
    return a + pltpu.roll(a, QK_ROPE, axis=1)


def _inproj_kernel(x_ref, mod_ref, gpre_ref, win_ref, wkT_ref, wa2_ref, ba_ref, wa2T_ref, baT_ref,
                   gq_ref, gkv_ref, wckvT_ref, gkvT_ref, wuq_ref, wukT_ref, cs_ref,
                   qa_ref, ka_ref, va_ref, ra_ref, la_ref, kT_ref, laT_ref,
                   qcat_ref, kcat_ref, latT_ref, lat_ref, kr_ref):
    bb, tt, d = x_ref.shape
    m = bb * tt
    x = x_ref[...]
    mod = mod_ref[...]
    sh1, sc1 = mod[:, :, 0:d], mod[:, :, d:2 * d]
    h = _rms(x, gpre_ref[...]) * (1.0 + sc1) + sh1
    h = h.reshape(m, d).astype(BF16)
    z = _mm(h, win_ref[...])

    qa_ref[...] = (z[:, _O_QA:_O_QA + HK] * (DK_A ** -0.5)).reshape(bb, tt, HK)
    ka_ref[...] = z[:, _O_KA:_O_KA + HK].reshape(bb, tt, HK)
    va_ref[...] = z[:, _O_VA:_O_VA + HV].astype(va_ref.dtype).reshape(bb, tt, HV)
    ra_ref[...] = z[:, _O_RA:_O_RA + HV].astype(ra_ref.dtype).reshape(bb, tt, HV)
    gr = z[:, _O_GR:_O_GR + LANE].astype(BF16)
    la = _log_sigmoid(_mm(gr, wa2_ref[...]) + ba_ref[...]) * (1.0 / GATE_TAU)
    la_ref[...] = la.reshape(bb, tt, HK)
    kT_ref[...] = _mm_nt(wkT_ref[...], h)
    laT_ref[...] = _log_sigmoid(_mm_nt(wa2T_ref[...], gr) + baT_ref[...]) * (1.0 / GATE_TAU)

    cs = cs_ref[...]
    ckv = _rms(z[:, _O_CKV:_O_CKV + KV_LORA], gkv_ref[...])
    lat_ref[...] = ckv.reshape(bb, tt, KV_LORA)
    krs = _rope_sum(z[:, _O_KRR:_O_KRR + 2 * QK_ROPE].reshape(bb, tt, 2 * QK_ROPE), cs)
    kr_ref[...] = krs[:, 0:QK_ROPE].reshape(bb, tt, QK_ROPE)
    ckv_t = _mm_nt(wckvT_ref[...], h)
    ckv_t = ckv_t * lax.rsqrt(jnp.mean(ckv_t * ckv_t, axis=0, keepdims=True) + EPS) * gkvT_ref[...]
    tkb = latT_ref.shape[2]
    for c in range(latT_ref.shape[0]):
        latT_ref[c] = ckv_t[:, c * tkb:(c + 1) * tkb].astype(latT_ref.dtype)
    lane = lax.broadcasted_iota(jnp.int32, (m, 2 * QK_ROPE), 1)
    kcat_ref[:, :, 0:KV_LORA] = ckv.astype(kcat_ref.dtype).reshape(bb, tt, KV_LORA)
    kcat_ref[:, :, KV_LORA:QCAT] = (
        jnp.where(lane < QK_ROPE, krs, 0.0).astype(kcat_ref.dtype).reshape(bb, tt, 2 * QK_ROPE))

    cq = _rms(z[:, _O_CQ:_O_CQ + Q_LORA], gq_ref[...]).astype(BF16)
    qb = _mm(cq, wuq_ref[...])
    for hh in range(H_B):
        qn = qb[:, hh * QK_NOPE:(hh + 1) * QK_NOPE].astype(BF16)
        qlat = _mm(qn, wukT_ref[hh])
        qcat_ref[:, hh, :, 0:KV_LORA] = (qlat * QK_SCALE_LOG2E).astype(qcat_ref.dtype).reshape(bb, tt, KV_LORA)
        o = H_B * QK_NOPE + hh * 2 * QK_ROPE
        qrs = _rope_sum(qb[:, o:o + 2 * QK_ROPE].reshape(bb, tt, 2 * QK_ROPE), cs) * QK_SCALE_LOG2E
        qcat_ref[:, hh, :, KV_LORA:QCAT] = qrs.astype(qcat_ref.dtype).reshape(bb, tt, 2 * QK_ROPE)


def _inproj(x, mod3, cs, w, *, bb, tt, tkb, act_dtype):
    nb, t, d = x.shape
    m = bb * tt
    grid = (nb // bb, t // tt)
    if bb == 1:
        t_shape = jax.ShapeDtypeStruct((nb, HK, t), F32)
        tsp = pl.BlockSpec((None, HK, tt), lambda i, j: (i, 0, j))
    else:
        t_shape = jax.ShapeDtypeStruct((HK, nb * t), F32)
        tsp = pl.BlockSpec((HK, nb * t), lambda i, j: (0, 0))
    tok = lambda width: pl.BlockSpec((bb, tt, width), lambda i, j: (i, j, 0))
    out_shape = (
        jax.ShapeDtypeStruct((nb, t, HK), F32),
        jax.ShapeDtypeStruct((nb, t, HK), F32),
        jax.ShapeDtypeStruct((nb, t, HV), act_dtype),
        jax.ShapeDtypeStruct((nb, t, HV), act_dtype),
        jax.ShapeDtypeStruct((nb, t, HK), F32),
        t_shape,
        t_shape,
        jax.ShapeDtypeStruct((nb, H_B, t, QCAT), act_dtype),
        jax.ShapeDtypeStruct((nb, t, QCAT), act_dtype),
        jax.ShapeDtypeStruct((nb // bb, (t // tt) * (m // tkb), KV_LORA, tkb), BF16),
        jax.ShapeDtypeStruct((nb, t, KV_LORA), F32),
        jax.ShapeDtypeStruct((nb, t, QK_ROPE), F32),
    )
    out_specs = (tok(HK), tok(HK), tok(HV), tok(HV), tok(HK), tsp, tsp,
                 pl.BlockSpec((bb, H_B, tt, QCAT), lambda i, j: (i, 0, j, 0)),
                 tok(QCAT),
                 pl.BlockSpec((None, m // tkb, KV_LORA, tkb), lambda i, j: (i, j, 0, 0)),
                 tok(KV_LORA), tok(QK_ROPE))
    in_specs = [
        tok(d),
        pl.BlockSpec((bb, 1, mod3.shape[2]), lambda i, j: (i, 0, 0)),
        _const_spec((1, d)),
        _const_spec(w["win"].shape), _const_spec(w["wkT"].shape),
        _const_spec(w["wa2"].shape), _const_spec((1, HK)),
        _const_spec(w["wa2T"].shape), _const_spec((HK, 1)),
        _const_spec((1, Q_LORA)), _const_spec((1, KV_LORA)),
        _const_spec(w["wckvT"].shape), _const_spec((KV_LORA, 1)),
        _const_spec(w["wuq"].shape), _const_spec(w["wukT"].shape),
        pl.BlockSpec((tt, 2 * QK_ROPE), lambda i, j: (j, 0)),
    ]
    return pl.pallas_call(
        _inproj_kernel, out_shape=out_shape, grid=grid, in_specs=in_specs, out_specs=out_specs,
        compiler_params=_cparams(("arbitrary", "arbitrary")), name="inproj",
    )(x, mod3, w["g_pre_mix"], w["win"], w["wkT"], w["wa2"], w["ba"], w["wa2T"], w["baT"],
      w["g_mla_q"], w["g_mla_kv"], w["wckvT"], w["g_mla_kvT"], w["wuq"], w["wukT"], cs)


def _gla_kernel(q_ref, k_ref, la_ref, kT_ref, laT_ref, v_ref, r_ref, s0_ref, g_ref,
                o_ref, sfin_ref, s_ref, *, chunk, sub):
    j = pl.program_id(1)
    c_, r_ = chunk, sub
    nsub = c_ // r_
    tc = q_ref.shape[1]

    @pl.when(j == 0)
    def _():
        s_ref[...] = s0_ref[0]

    row = lax.broadcasted_iota(jnp.int32, (2 * c_, c_), 0)
    col = lax.broadcasted_iota(jnp.int32, (2 * c_, c_), 1)
    bound = jnp.where(row < c_, row + 1, ((row - c_) // r_) * r_)
    lmat = jnp.where(col < bound, 1.0, 0.0).astype(BF16)
    ur = lax.broadcasted_iota(jnp.int32, (c_, c_), 0)
    uc = lax.broadcasted_iota(jnp.int32, (c_, c_), 1)
    umat = jnp.where(ur <= uc, 1.0, 0.0).astype(BF16)
    lane = lax.broadcasted_iota(jnp.int32, (1, HK), 1)
    hmask = [(lane >= hh * DK_A) & (lane < (hh + 1) * DK_A) for hh in range(H_A)]
    g_out = g_ref[...]

    def stack_heads(a):
        return jnp.concatenate([jnp.where(hmask[hh], a, 0.0) for hh in range(H_A)], axis=0).astype(BF16)

    for c in range(tc // c_):
        sl = slice(c * c_, (c + 1) * c_)
        g1, g2, g3 = _split3(la_ref[0, sl, :])
        bm = _mm(lmat, g1) + _mm(lmat, g2) + _mm(lmat, g3)
        b, mrow = bm[:c_], bm[c_:]
        t1, t2, t3 = _split3(laT_ref[0, :, sl])
        bT = _mm(t1, umat) + _mm(t2, umat) + _mm(t3, umat)
        q = q_ref[0, sl, :]
        k = k_ref[0, sl, :]
        v = v_ref[0, sl, :].astype(BF16)
        s_old = s_ref[...]

        o_inter = _mm(stack_heads(q * jnp.exp(b)), s_old.astype(BF16))

        qt = q * jnp.exp(b - mrow)
        o_sub = [[] for _ in range(H_A)]
        for i in range(nsub):
            n_k = (i + 1) * r_
            m_i = mrow[i * r_:i * r_ + 1, :]
            kt = (k[:n_k] * jnp.exp(jnp.minimum(m_i - b[:n_k], GLA_EXP_CLAMP))).astype(BF16)
            att = _mm_nt(stack_heads(qt[i * r_:(i + 1) * r_]), kt)
            rr = lax.broadcasted_iota(jnp.int32, (H_A * r_, n_k), 0) % r_
            cc = lax.broadcasted_iota(jnp.int32, (H_A * r_, n_k), 1)
            att = jnp.where(cc <= rr + i * r_, att, 0.0).astype(BF16)
            pv = _mm(att, v[:n_k])
            for hh in range(H_A):
                o_sub[hh].append(pv[hh * r_:(hh + 1) * r_, hh * DV_A:(hh + 1) * DV_A])
        for hh in range(H_A):
            o_h = o_inter[hh * c_:(hh + 1) * c_] + jnp.concatenate(o_sub[hh], axis=0)
            gate = r_ref[0, sl, hh * DV_A:(hh + 1) * DV_A].astype(F32)
            o_ref[0, sl, hh * DV_A:(hh + 1) * DV_A] = (_rms(o_h, g_out) * _silu(gate)).astype(o_ref.dtype)

        bl = bT[:, c_ - 1:c_]
        klT = (kT_ref[0, :, sl] * jnp.exp(bl - bT)).astype(BF16)
        upd = _mm(klT, v)
        decay = jnp.exp(bl)
        s_ref[...] = jnp.concatenate(
            [s_old[hh * DK_A:(hh + 1) * DK_A] * decay[hh * DK_A:(hh + 1) * DK_A]
             + upd[hh * DK_A:(hh + 1) * DK_A, hh * DV_A:(hh + 1) * DV_A] for hh in range(H_A)], axis=0)

    @pl.when(j == pl.num_programs(1) - 1)
    def _():
        sfin_ref[0] = s_ref[...]


def _gla(qa, ka, la, kT, laT, va, ra, s0, g_out, *, chunk, sub, tc, out_dtype):
    nb, t, _ = qa.shape
    grid = (nb, t // tc)
    tok = lambda width: pl.BlockSpec((1, tc, width), lambda b, j: (b, j, 0))
    tsp = pl.BlockSpec((1, HK, tc), lambda b, j: (b, 0, j))
    st = pl.BlockSpec((1, HK, DV_A), lambda b, j: (b, 0, 0))
    return pl.pallas_call(
        functools.partial(_gla_kernel, chunk=chunk, sub=sub),
        out_shape=(jax.ShapeDtypeStruct((nb, t, HV), out_dtype),
                   jax.ShapeDtypeStruct((nb, HK, DV_A), F32)),
        grid=grid,
        in_specs=[tok(HK), tok(HK), tok(HK), tsp, tsp, tok(HV), tok(HV), st, _const_spec((1, DV_A))],
        out_specs=(tok(HV), st),
        scratch_shapes=[pltpu.VMEM((HK, DV_A), F32)],
        compiler_params=_cparams(("arbitrary", "arbitrary")), name="gla",
    )(qa, ka, la, kT, laT, va, ra, s0, g_out)


def _flash_kernel(q_ref, k_ref, vT_ref, wuvT_ref, g_ref, o_ref, m_ref, l_ref, acc_ref):
    qi = pl.program_id(1)
    tq = q_ref.shape[2]
    rows = H_B * tq
    q = q_ref[0].reshape(rows, QCAT)
    m_ref[...] = jnp.full(m_ref.shape, -jnp.inf, F32)
    l_ref[...] = jnp.zeros(l_ref.shape, F32)
    acc_ref[...] = jnp.zeros(acc_ref.shape, F32)

    def update(j, diagonal):
        kc = k_ref[0, pl.ds(pl.multiple_of(j * tq, tq), tq), :]
        st = _mm_nt(kc, q)
        if diagonal:
            t_k = lax.broadcasted_iota(jnp.int32, (tq, rows), 0)
            t_q = lax.broadcasted_iota(jnp.int32, (tq, rows), 1) % tq
            st = jnp.where(t_k <= t_q, st, -jnp.inf)
        m_prev = m_ref[...]
        m_new = jnp.maximum(m_prev, jnp.max(st, axis=0, keepdims=True))
        alpha = jnp.exp2(m_prev - m_new)
        p = jnp.exp2(st - m_new)
        l_ref[...] = alpha * l_ref[...] + jnp.sum(p, axis=0, keepdims=True)
        acc_ref[...] = alpha * acc_ref[...] + _mm(vT_ref[0, j], p.astype(BF16))
        m_ref[...] = m_new

    def full_block(j, carry):
        update(j, False)
        return carry

    lax.fori_loop(0, qi, full_block, 0)
    update(qi, True)

    o_lat_t = (acc_ref[...] / l_ref[...]).astype(BF16)
    for hh in range(H_B):
        ob_t = _mm(wuvT_ref[hh], o_lat_t[:, hh * tq:(hh + 1) * tq])
        y_t = ob_t * lax.rsqrt(jnp.mean(ob_t * ob_t, axis=0, keepdims=True) + EPS) * g_ref[...]
        o_ref[0, :, hh * V_B:(hh + 1) * V_B] = y_t.T.astype(o_ref.dtype)


def _flash(qcat, kcat, lat_t, wuv_t, g_col, *, tq):
    nb, _, t, _ = qcat.shape
    rows = H_B * tq
    return pl.pallas_call(
        _flash_kernel,
        out_shape=jax.ShapeDtypeStruct((nb, t, H_B * V_B), BF16),
        grid=(nb, t // tq),
        in_specs=[pl.BlockSpec((1, H_B, tq, QCAT), lambda b, i: (b, 0, i, 0)),
                  pl.BlockSpec((1, t, QCAT), lambda b, i: (b, 0, 0)),
                  pl.BlockSpec((1, t // tq, KV_LORA, tq), lambda b, i: (b, 0, 0, 0)),
                  _const_spec((H_B, V_B, KV_LORA)), _const_spec((V_B, 1))],
        out_specs=pl.BlockSpec((1, tq, H_B * V_B), lambda b, i: (b, i, 0)),
        scratch_shapes=[pltpu.VMEM((1, rows), F32), pltpu.VMEM((1, rows), F32),
                        pltpu.VMEM((KV_LORA, rows), F32)],
        compiler_params=_cparams(("arbitrary", "arbitrary")), name="flash",
    )(qcat, kcat, lat_t, wuv_t, g_col)


def _decode_kernel(pt_ref, q_ref, knew_ref, wuv_ref, g_ref, lat_hbm, krT_hbm, o_ref,
                   lat_buf, kr_buf, s_scr, p_scr, sems, *, n_pages, td, ck):
    b = pl.program_id(0)
    slot = b % 2
    page = lat_hbm.shape[1]
    nck = (n_pages * page) // ck
    ppc = ck // page

    def page_copies(seq, sl, p):
        pg = pt_ref[seq * n_pages + p]
        rows = pl.ds(pl.multiple_of(p * page, page), page)
        return (pltpu.make_async_copy(lat_hbm.at[pg], lat_buf.at[sl, rows, :], sems.at[0, sl]),
                pltpu.make_async_copy(krT_hbm.at[pg], kr_buf.at[sl, p], sems.at[1, sl]))

    def fetch(seq, sl):
        def body(p, carry):
            for cp in page_copies(seq, sl, p):
                cp.start()
            return carry
        lax.fori_loop(0, n_pages, body, 0)

    @pl.when(b == 0)
    def _():
        fetch(0, 0)

    @pl.when(b + 1 < pl.num_programs(0))
    def _():
        fetch(b + 1, 1 - slot)

    pltpu.make_async_copy(lat_buf.at[slot], lat_buf.at[slot], sems.at[0, slot]).wait()
    pltpu.make_async_copy(kr_buf.at[slot], kr_buf.at[slot], sems.at[1, slot]).wait()

    q = q_ref[0].astype(BF16)
    ql, qr = q[:, 0:KV_LORA], q[:, KV_LORA:KV_LORA + QK_ROPE]

    def lat_chunk(c):
        return lat_buf[slot, pl.ds(pl.multiple_of(c * ck, ck), ck), :].astype(BF16)

    def scores(c, carry):
        s_rope = jnp.concatenate(
            [_mm(qr, kr_buf[slot, c * ppc + i].astype(BF16)) for i in range(ppc)], axis=1)
        s_scr[c] = _mm_nt(ql, lat_chunk(c)) + s_rope
        return carry
    lax.fori_loop(0, nck, scores, 0)

    kn = knew_ref[0].astype(BF16)
    s_new = _mm_nt(q, kn)
    t_q = lax.broadcasted_iota(jnp.int32, s_new.shape, 0) % td
    t_k = lax.broadcasted_iota(jnp.int32, s_new.shape, 1)
    s_new = jnp.where(t_k <= t_q, s_new, -jnp.inf)
    s_all = s_scr[...]
    m = jnp.maximum(jnp.max(jnp.max(s_all, axis=0), axis=-1, keepdims=True),
                    jnp.max(s_new, axis=-1, keepdims=True))
    p_all = jnp.exp2(s_all - m[None])
    p_new = jnp.exp2(s_new - m)
    l = jnp.sum(jnp.sum(p_all, axis=0), axis=-1, keepdims=True) + jnp.sum(p_new, axis=-1, keepdims=True)
    p_scr[...] = p_all.astype(BF16)

    def weighted(c, acc):
        return acc + _mm(p_scr[c], lat_chunk(c))
    acc = lax.fori_loop(0, nck, weighted, _mm(p_new.astype(BF16), kn[:, 0:KV_LORA]))
    o_lat = (acc / l).astype(BF16)
    ob = _mm(o_lat, wuv_ref[...])
    for hh in range(H_B):
        blk = ob[hh * td:(hh + 1) * td, hh * V_B:(hh + 1) * V_B]
        o_ref[0, :, hh * V_B:(hh + 1) * V_B] = _rms(blk, g_ref[...]).astype(o_ref.dtype)


def _decode(qcat, knew, wuv_all, g_out, cache_lat, cache_krT, page_table, *, ck):
    nb, _, td, _ = qcat.shape
    qcat = qcat.reshape(nb, H_B * td, QCAT)
    n_pages = page_table.shape[1]
    page = cache_lat.shape[1]
    past = n_pages * page
    rows = H_B * td
    grid_spec = pltpu.PrefetchScalarGridSpec(
        num_scalar_prefetch=1,
        grid=(nb,),
        in_specs=[pl.BlockSpec((1, rows, QCAT), lambda b, pt: (b, 0, 0)),
                  pl.BlockSpec((1, LANE, QCAT), lambda b, pt: (b, 0, 0)),
                  pl.BlockSpec(wuv_all.shape, lambda b, pt: (0, 0)),
                  pl.BlockSpec((1, V_B), lambda b, pt: (0, 0)),
                  pl.BlockSpec(memory_space=pl.ANY),
                  pl.BlockSpec(memory_space=pl.ANY)],
        out_specs=pl.BlockSpec((1, td, H_B * V_B), lambda b, pt: (b, 0, 0)),
        scratch_shapes=[pltpu.VMEM((2, past, KV_LORA), F32),
                        pltpu.VMEM((2, n_pages, QK_ROPE, page), F32),
                        pltpu.VMEM((past // ck, rows, ck), F32),
                        pltpu.VMEM((past // ck, rows, ck), BF16),
                        pltpu.SemaphoreType.DMA((2, 2))],
    )
    return pl.pallas_call(
        functools.partial(_decode_kernel, n_pages=n_pages, td=td, ck=ck),
        out_shape=jax.ShapeDtypeStruct((nb, td, H_B * V_B), F32),
        grid_spec=grid_spec,
        compiler_params=_cparams(("arbitrary",)), name="decode",
    )(page_table.reshape(-1), qcat, knew, wuv_all, g_out, cache_lat, cache_krT)


def _outproj_kernel(oa_ref, ob_ref, x_ref, mod_ref, wo_ref, gpost_ref, gpre_ref, x1_ref, h2_ref):
    bb, tt, d = x_ref.shape
    m = bb * tt
    mod = mod_ref[...]
    gt1, sh2, sc2 = mod[:, :, 2 * d:3 * d], mod[:, :, 3 * d:4 * d], mod[:, :, 4 * d:5 * d]
    mix = (_mm(oa_ref[...].reshape(m, HV).astype(BF16), wo_ref[0:HV, :])
           + _mm(ob_ref[...].reshape(m, H_B * V_B).astype(BF16), wo_ref[HV:HV + H_B * V_B, :]))
    x1 = x_ref[...] + gt1 * _rms(mix, gpost_ref[...]).reshape(bb, tt, d)
    x1_ref[...] = x1
    h2_ref[...] = (_rms(x1, gpre_ref[...]) * (1.0 + sc2) + sh2).astype(h2_ref.dtype)


def _outproj(oa, ob, x, mod3, w, *, bb, tt, act_dtype):
    nb, t, d = x.shape
    grid = (nb // bb, t // tt)
    tok = lambda width: pl.BlockSpec((bb, tt, width), lambda i, j: (i, j, 0))
    return pl.pallas_call(
        _outproj_kernel,
        out_shape=(jax.ShapeDtypeStruct((nb, t, d), F32), jax.ShapeDtypeStruct((nb, t, d), act_dtype)),
        grid=grid,
        in_specs=[tok(HV), tok(H_B * V_B), tok(d),
                  pl.BlockSpec((bb, 1, mod3.shape[2]), lambda i, j: (i, 0, 0)),
                  _const_spec(w["wo"].shape), _const_spec((1, d)), _const_spec((1, d))],
        out_specs=(tok(d), tok(d)),
        compiler_params=_cparams(("arbitrary", "arbitrary")), name="outproj",
    )(oa, ob, x, mod3, w["wo"], w["g_post_mix"], w["g_pre_ffn"])


def _gelu_tanh(x):
    return 0.5 * x * (1.0 + jnp.tanh(math.sqrt(2.0 / math.pi) * (x + 0.044715 * (x * x * x))))


def _ffn_kernel(h2_ref, x1_ref, mod_ref, cst_ref, wup_ref, wc_ref, bc_ref, wdn_ref, gpost_ref,
                y_ref, cnew_ref, u_ref, *, fchunk):
    bb, tt, d = x1_ref.shape
    m = bb * tt
    ff = wdn_ref.shape[0]
    j = pl.program_id(1)
    pad = SUBLANE

    @pl.when(j == 0)
    def _():
        u_ref[:, pad - (CONV_W - 1):pad, :] = cst_ref[...]

    h2 = h2_ref[...].reshape(m, d).astype(BF16)
    acc = jnp.zeros((m, d), F32)
    for c in range(ff // fchunk):
        halves = []
        for off in (c * fchunk, ff + c * fchunk):
            cols = slice(off, off + fchunk)
            u_ref[:, pad:pad + tt, cols] = _mm(h2, wup_ref[:, cols]).reshape(bb, tt, fchunk)
            uc = bc_ref[:, cols][None]
            for tap in range(CONV_W):
                lo = pad - (CONV_W - 1) + tap
                uc = uc + wc_ref[tap:tap + 1, cols][None] * u_ref[:, lo:lo + tt, cols]
            halves.append(uc.reshape(m, fchunk))
        act = (halves[0] * _gelu_tanh(halves[1])).astype(BF16)
        acc = acc + _mm(act, wdn_ref[c * fchunk:(c + 1) * fchunk, :])
    gt2 = mod_ref[...][:, :, 5 * d:6 * d]
    y_ref[...] = x1_ref[...] + gt2 * _rms(acc, gpost_ref[...]).reshape(bb, tt, d)
    tail = u_ref[:, pad + tt - (CONV_W - 1):pad + tt, :]
    cnew_ref[...] = tail
    u_ref[:, pad - (CONV_W - 1):pad, :] = tail


def _ffn(h2, x1, mod3, cstate, w, *, bb, tt, fchunk):
    nb, t, d = x1.shape
    f2 = w["wup"].shape[1]
    grid = (nb // bb, t // tt)
    tok = lambda width: pl.BlockSpec((bb, tt, width), lambda i, j: (i, j, 0))
    cs_spec = pl.BlockSpec((bb, CONV_W - 1, f2), lambda i, j: (i, 0, 0))
    return pl.pallas_call(
        functools.partial(_ffn_kernel, fchunk=fchunk),
        out_shape=(jax.ShapeDtypeStruct((nb, t, d), F32),
                   jax.ShapeDtypeStruct((nb, CONV_W - 1, f2), F32)),
        grid=grid,
        in_specs=[tok(d), tok(d),
                  pl.BlockSpec((bb, 1, mod3.shape[2]), lambda i, j: (i, 0, 0)),
                  cs_spec,
                  _const_spec(w["wup"].shape), _const_spec((CONV_W, f2)), _const_spec((1, f2)),
                  _const_spec(w["wdn"].shape), _const_spec((1, d))],
        out_specs=(tok(d), cs_spec),
        scratch_shapes=[pltpu.VMEM((bb, SUBLANE + tt, f2), F32)],
        compiler_params=_cparams(("arbitrary", "arbitrary")), name="ffn",
    )(h2, x1, mod3, cstate, w["wup"], w["wconv"], w["bconv"], w["wdn"], w["g_post_ffn"])


def _rot_cols(wr):
    half = wr.shape[-1] // 2
    return jnp.concatenate([-wr[..., half:], wr[..., :half]], axis=-1)


def _prep_weights(l, w_in, w_gla_a2, b_gla_a, g_gla_out, g_mla_q, g_mla_kv, w_mla_uq, w_mla_uk, w_mla_uv,
                  g_mla_out, w_o, w_ffn_up, w_ffn_conv, b_ffn_conv, w_ffn_down, g_pre_mix, g_post_mix,
                  g_pre_ffn, g_post_ffn):
    wi = w_in[l]
    d = wi.shape[0]
    o_gr = 2 * HK + 2 * HV
    o_cq = o_gr + GATE_RANK
    o_ckv = o_cq + Q_LORA
    o_kr = o_ckv + KV_LORA
    kr = wi[:, o_kr:o_kr + QK_ROPE]
    win = jnp.concatenate(
        [wi[:, 0:o_gr], wi[:, o_cq:o_ckv], wi[:, o_ckv:o_kr], kr, _rot_cols(kr),
         wi[:, o_gr:o_cq], jnp.zeros((d, LANE - GATE_RANK), wi.dtype)], axis=1).astype(BF16)
    wa2 = jnp.concatenate([w_gla_a2[l], jnp.zeros((LANE - GATE_RANK, HK), F32)], axis=0).astype(BF16)
    uq = w_mla_uq[l].reshape(Q_LORA, H_B, QK_NOPE + QK_ROPE)
    uq_rope = uq[:, :, QK_NOPE:]
    wuq = jnp.concatenate(
        [uq[:, :, :QK_NOPE].reshape(Q_LORA, H_B * QK_NOPE),
         jnp.concatenate([uq_rope, _rot_cols(uq_rope)], axis=-1).reshape(Q_LORA, H_B * 2 * QK_ROPE)],
        axis=1).astype(BF16)
    return dict(
        win=win, wkT=wi[:, HK:2 * HK].T.astype(BF16),
        wa2=wa2, ba=b_gla_a[l].reshape(1, HK), wa2T=wa2.T, baT=b_gla_a[l].reshape(HK, 1),
        g_gla_out=g_gla_out[l].reshape(1, DV_A),
        g_mla_q=g_mla_q[l].reshape(1, Q_LORA), g_mla_kv=g_mla_kv[l].reshape(1, KV_LORA),
        wuq=wuq, wukT=jnp.transpose(w_mla_uk[l], (1, 2, 0)).astype(BF16),
        wckvT=wi[:, o_ckv:o_kr].T.astype(BF16), g_mla_kvT=g_mla_kv[l].reshape(KV_LORA, 1),
        wuvT=jnp.transpose(w_mla_uv[l], (1, 2, 0)).astype(BF16),
        g_mla_outT=g_mla_out[l].reshape(V_B, 1),
        wuv_all=w_mla_uv[l].reshape(KV_LORA, H_B * V_B).astype(BF16),
        g_mla_out=g_mla_out[l].reshape(1, V_B),
        wo=w_o[l].astype(BF16), wup=w_ffn_up[l].astype(BF16), wconv=w_ffn_conv[l],
        bconv=b_ffn_conv[l].reshape(1, -1), wdn=w_ffn_down[l].astype(BF16),
        g_pre_mix=g_pre_mix[l].reshape(1, d), g_post_mix=g_post_mix[l].reshape(1, d),
        g_pre_ffn=g_pre_ffn[l].reshape(1, d), g_post_ffn=g_post_ffn[l].reshape(1, d),
    )


def _rope_table(pos):
    inv = ROPE_BASE ** (-jnp.arange(0, QK_ROPE, 2, dtype=F32) / QK_ROPE)
    ang = pos.astype(F32)[:, None] * inv[None, :]
    cos, sin = jnp.cos(ang), jnp.sin(ang)
    return jnp.concatenate([cos, cos, sin, sin], axis=-1)


def _token_tile(t, target):
    tt = min(t, target)
    while t % tt:
        tt //= 2
    return tt


def _layer(x, mod3, cs, w, gla_s0, conv_state, *, per_seq, attend):
    nb, t, d = x.shape
    if per_seq:
        bb, tt = nb, t
    else:
        bb, tt = 1, _token_tile(t, 512)
    act_dtype = F32 if per_seq else BF16
    tkb = _token_tile(bb * tt, ATTN_TILE)
    qa, ka, va, ra, la, kT, laT, qcat, kcat, lat_t, lat, kr = _inproj(
        x, mod3, cs, w, bb=bb, tt=tt, tkb=tkb, act_dtype=act_dtype)

    if per_seq:
        tp = 2 * SUBLANE
        padt = lambda a: jnp.pad(a, ((0, 0), (0, tp - t), (0, 0)))
        tr = lambda a: jnp.pad(a.reshape(HK, nb, t).transpose(1, 0, 2), ((0, 0), (0, 0), (0, tp - t)))
        o_a, s_fin = _gla(padt(qa), padt(ka), padt(la), tr(kT), tr(laT), padt(va), padt(ra), gla_s0,
                          w["g_gla_out"], chunk=tp, sub=tp, tc=tp, out_dtype=act_dtype)
        o_a = o_a[:, :t]
    else:
        o_a, s_fin = _gla(qa, ka, la, kT, laT, va, ra, gla_s0, w["g_gla_out"],
                          chunk=LANE, sub=LANE // 4, tc=_token_tile(t, 256), out_dtype=act_dtype)

    o_b = attend(qcat, kcat, lat_t)
    x1, h2 = _outproj(o_a, o_b, x, mod3, w, bb=bb, tt=tt, act_dtype=act_dtype)
    ffn_tt = tt if per_seq else _token_tile(t, 256)
    y, c_new = _ffn(h2, x1, mod3, conv_state, w, bb=bb, tt=ffn_tt, fchunk=w["wdn"].shape[0] // 2)
    return y, lat, kr, s_fin, c_new


def kernel(x_prompt, x_sample, c_prompt, c_sample, cache_latent, cache_krope, state_gla, state_conv, page_table, w_ada, b_ada, g_pre_mix, g_post_mix, g_pre_ffn, g_post_ffn, w_in, w_gla_a2, b_gla_a, g_gla_out, g_mla_q, g_mla_kv, w_mla_uq, w_mla_uk, w_mla_uv, g_mla_out, w_o, w_ffn_up, w_ffn_conv, b_ffn_conv, w_ffn_down):
    depth = w_ada.shape[0]
    nbp, tp, d = x_prompt.shape
    nbs, ts, _ = x_sample.shape
    n_pages, page = page_table.shape[1], cache_latent.shape[2]
    past_len = n_pages * page
    f2 = w_ffn_up.shape[2]
    cs_p = _rope_table(jnp.arange(tp, dtype=jnp.int32))
    cs_s = _rope_table(past_len + jnp.arange(ts, dtype=jnp.int32))

    hp, hs = x_prompt, x_sample
    outs = [[] for _ in range(8)]
    for l in range(depth):
        w = _prep_weights(l, w_in, w_gla_a2, b_gla_a, g_gla_out, g_mla_q, g_mla_kv, w_mla_uq, w_mla_uk,
                          w_mla_uv, g_mla_out, w_o, w_ffn_up, w_ffn_conv, b_ffn_conv, w_ffn_down,
                          g_pre_mix, g_post_mix, g_pre_ffn, g_post_ffn)
        mod = _ada(jnp.concatenate([c_prompt, c_sample], axis=0), w_ada[l], b_ada[l])
        mod3 = mod.reshape(nbp + nbs, 1, mod.shape[1])

        def attend_prompt(qcat, kcat, lat_t):
            return _flash(qcat, kcat, lat_t, w["wuvT"], w["g_mla_outT"], tq=_token_tile(tp, ATTN_TILE))

        def attend_sample(qcat, kcat, lat_t):
            knew = jnp.pad(kcat, ((0, 0), (0, LANE - ts), (0, 0)))
            return _decode(qcat, knew, w["wuv_all"], w["g_mla_out"], cache_latent[l],
                           jnp.swapaxes(cache_krope[l], 1, 2), page_table, ck=8 * page)

        hp, a1, a2, a3, a4 = _layer(
            hp, mod3[:nbp], cs_p, w, jnp.zeros((nbp, HK, DV_A), F32),
            jnp.zeros((nbp, CONV_W - 1, f2), F32), per_seq=False, attend=attend_prompt)
        hs, b1, b2, b3, b4 = _layer(
            hs, mod3[nbp:], cs_s, w, state_gla[l].reshape(nbs, HK, DV_A), state_conv[l],
            per_seq=True, attend=attend_sample)
        for lst, val in zip(outs, (a1, a2, a3.reshape(nbp, H_A, DK_A, DV_A), a4,
                                   b1, b2, b3.reshape(nbs, H_A, DK_A, DV_A), b4)):
            lst.append(val)
    return (hp, hs) + tuple(jnp.stack(o) for o in outs)
```

```python
import functools
import math

import jax
import jax.numpy as jnp
import numpy as np
from jax import lax
from jax.experimental import pallas as pl
from jax.experimental.pallas import tpu as pltpu

F32 = jnp.float32
BF16 = jnp.bfloat16

EPS = 1e-6
H_A, DK_A, DV_A = 4, 64, 128
GATE_RANK = 16
GATE_TAU = 16.0
H_B = 4
Q_LORA, KV_LORA = 384, 256
QK_NOPE, QK_ROPE, V_B = 128, 64, 128
ROPE_BASE = 10000.0
CONV_W = 3

HK = H_A * DK_A
HV = H_A * DV_A
QCAT = KV_LORA + 2 * QK_ROPE
LANE = 128
SUBLANE = 8

_O_QA, _O_KA, _O_VA, _O_RA = 0, HK, 2 * HK, 2 * HK + HV
_O_CQ = 2 * HK + 2 * HV
_O_CKV = _O_CQ + Q_LORA
_O_KRR = _O_CKV + KV_LORA
_O_GR = _O_KRR + 2 * QK_ROPE
IN_EXT = _O_GR + LANE

QK_SCALE_LOG2E = (QK_NOPE + QK_ROPE) ** -0.5 * math.log2(math.e)

ATTN_TILE = 256
FFN_CHUNK = 256
FFN_ROWS = 256
FFN_AHEAD = 4
VMEM_LIMIT = 56 * 1024 * 1024
GLA_EXP_CLAMP = 80.0


def _mm(a, b):
    return jnp.dot(a, b, preferred_element_type=F32)


def _mm_nt(a, b):
    return lax.dot_general(a, b, (((1,), (1,)), ((), ())), preferred_element_type=F32)


def _rms(x, g):
    return x * lax.rsqrt(jnp.mean(x * x, axis=-1, keepdims=True) + EPS) * g


def _silu(x):
    return x / (1.0 + jnp.exp(-x))


def _log_sigmoid(x):
    return -(jnp.maximum(-x, 0.0) + jnp.log(1.0 + jnp.exp(-jnp.abs(x))))


def _split3(x):
    x1 = x.astype(BF16)
    r1 = x - x1.astype(F32)
    x2 = r1.astype(BF16)
    x3 = (r1 - x2.astype(F32)).astype(BF16)
    return x1, x2, x3


def _cparams(sem):
    return pltpu.CompilerParams(dimension_semantics=sem, vmem_limit_bytes=VMEM_LIMIT)


def _const_spec(shape):
    nd = len(shape)
    return pl.BlockSpec(shape, lambda *_: (0,) * nd)


def _ada_kernel(c_ref, w_ref, b_ref, o_ref):
    s = _silu(c_ref[...]).astype(BF16)
    o_ref[...] = _mm(s, w_ref[...].astype(BF16)) + b_ref[...]


def _ada(c_all, w_ada, b_ada):
    n, d = c_all.shape
    nout = w_ada.shape[1]
    tn = d
    return pl.pallas_call(
        _ada_kernel,
        out_shape=jax.ShapeDtypeStruct((n, nout), F32),
        grid=(nout // tn,),
        in_specs=[pl.BlockSpec((n, d), lambda j: (0, 0)),
                  pl.BlockSpec((d, tn), lambda j: (0, j)),
                  pl.BlockSpec((1, tn), lambda j: (0, j))],
        out_specs=pl.BlockSpec((n, tn), lambda j: (0, j)),
        compiler_params=_cparams(("arbitrary",)),
        name="ada",
    )(c_all, w_ada, b_ada.reshape(1, nout))


def _rope_sum(a3, cs):
    bb, tt, w = a3.shape
    a = (a3 * cs[None]).reshape(bb * tt, w)
    return a + pltpu.roll(a, QK_ROPE, axis=1)


def _inproj_kernel(x_ref, mod_ref, gpre_ref, win_ref, wkT_ref, wa2_ref, ba_ref, wa2T_ref, baT_ref,
                   gq_ref, gkv_ref, wckvT_ref, gkvT_ref, wuq_ref, wukT_ref, cs_ref,
                   qa_ref, ka_ref, va_ref, ra_ref, la_ref, kT_ref, laT_ref,
                   qcat_ref, kcat_ref, latT_ref, lat_ref, kr_ref):
    bb, tt, d = x_ref.shape
    m = bb * tt
    x = x_ref[...]
    mod = mod_ref[...]
    sh1, sc1 = mod[:, :, 0:d], mod[:, :, d:2 * d]
    h = _rms(x, gpre_ref[...]) * (1.0 + sc1) + sh1
    h = h.reshape(m, d).astype(BF16)
    z = _mm(h, win_ref[...])

    qa_ref[...] = (z[:, _O_QA:_O_QA + HK] * (DK_A ** -0.5)).reshape(bb, tt, HK)
    ka_ref[...] = z[:, _O_KA:_O_KA + HK].reshape(bb, tt, HK)
    va_ref[...] = z[:, _O_VA:_O_VA + HV].astype(va_ref.dtype).reshape(bb, tt, HV)
    ra_ref[...] = z[:, _O_RA:_O_RA + HV].astype(ra_ref.dtype).reshape(bb, tt, HV)
    gr = z[:, _O_GR:_O_GR + LANE].astype(BF16)
    la = _log_sigmoid(_mm(gr, wa2_ref[...]) + ba_ref[...]) * (1.0 / GATE_TAU)
    la_ref[...] = la.reshape(bb, tt, HK)
    kT_ref[...] = _mm_nt(wkT_ref[...], h)
    laT_ref[...] = _log_sigmoid(_mm_nt(wa2T_ref[...], gr) + baT_ref[...]) * (1.0 / GATE_TAU)

    cs = cs_ref[...]
    ckv = _rms(z[:, _O_CKV:_O_CKV + KV_LORA], gkv_ref[...])
    lat_ref[...] = ckv.reshape(bb, tt, KV_LORA)
    krs = _rope_sum(z[:, _O_KRR:_O_KRR + 2 * QK_ROPE].reshape(bb, tt, 2 * QK_ROPE), cs)
    kr_ref[...] = krs[:, 0:QK_ROPE].reshape(bb, tt, QK_ROPE)
    ckv_t = _mm_nt(wckvT_ref[...], h)
    ckv_t = ckv_t * lax.rsqrt(jnp.mean(ckv_t * ckv_t, axis=0, keepdims=True) + EPS) * gkvT_ref[...]
    tkb = latT_ref.shape[2]
    for c in range(latT_ref.shape[0]):
        latT_ref[c] = ckv_t[:, c * tkb:(c + 1) * tkb].astype(latT_ref.dtype)
    lane = lax.broadcasted_iota(jnp.int32, (m, 2 * QK_ROPE), 1)
    kcat_ref[:, :, 0:KV_LORA] = ckv.astype(kcat_ref.dtype).reshape(bb, tt, KV_LORA)
    kcat_ref[:, :, KV_LORA:QCAT] = (
        jnp.where(lane < QK_ROPE, krs, 0.0).astype(kcat_ref.dtype).reshape(bb, tt, 2 * QK_ROPE))

    cq = _rms(z[:, _O_CQ:_O_CQ + Q_LORA], gq_ref[...]).astype(BF16)
    qb = _mm(cq, wuq_ref[...])
    for hh in range(H_B):
        qn = qb[:, hh * QK_NOPE:(hh + 1) * QK_NOPE].astype(BF16)
        qlat = _mm(qn, wukT_ref[hh])
        qcat_ref[:, hh, :, 0:KV_LORA] = (qlat * QK_SCALE_LOG2E).astype(qcat_ref.dtype).reshape(bb, tt, KV_LORA)
        o = H_B * QK_NOPE + hh * 2 * QK_ROPE
        qrs = _rope_sum(qb[:, o:o + 2 * QK_ROPE].reshape(bb, tt, 2 * QK_ROPE), cs) * QK_SCALE_LOG2E
        qcat_ref[:, hh, :, KV_LORA:QCAT] = qrs.astype(qcat_ref.dtype).reshape(bb, tt, 2 * QK_ROPE)


def _inproj(x, mod3, cs, w, *, bb, tt, tkb, act_dtype):
    nb, t, d = x.shape
    m = bb * tt
    grid = (nb // bb, t // tt)
    if bb == 1:
        t_shape = jax.ShapeDtypeStruct((nb, HK, t), F32)
        tsp = pl.BlockSpec((None, HK, tt), lambda i, j: (i, 0, j))
    else:
        t_shape = jax.ShapeDtypeStruct((HK, nb * t), F32)
        tsp = pl.BlockSpec((HK, nb * t), lambda i, j: (0, 0))
    tok = lambda width: pl.BlockSpec((bb, tt, width), lambda i, j: (i, j, 0))
    out_shape = (
        jax.ShapeDtypeStruct((nb, t, HK), F32),
        jax.ShapeDtypeStruct((nb, t, HK), F32),
        jax.ShapeDtypeStruct((nb, t, HV), act_dtype),
        jax.ShapeDtypeStruct((nb, t, HV), act_dtype),
        jax.ShapeDtypeStruct((nb, t, HK), F32),
        t_shape,
        t_shape,
        jax.ShapeDtypeStruct((nb, H_B, t, QCAT), act_dtype),
        jax.ShapeDtypeStruct((nb, t, QCAT), act_dtype),
        jax.ShapeDtypeStruct((nb // bb, (t // tt) * (m // tkb), KV_LORA, tkb), BF16),
        jax.ShapeDtypeStruct((nb, t, KV_LORA), F32),
        jax.ShapeDtypeStruct((nb, t, QK_ROPE), F32),
    )
    out_specs = (tok(HK), tok(HK), tok(HV), tok(HV), tok(HK), tsp, tsp,
                 pl.BlockSpec((bb, H_B, tt, QCAT), lambda i, j: (i, 0, j, 0)),
                 tok(QCAT),
                 pl.BlockSpec((None, m // tkb, KV_LORA, tkb), lambda i, j: (i, j, 0, 0)),
                 tok(KV_LORA), tok(QK_ROPE))
    in_specs = [
        tok(d),
        pl.BlockSpec((bb, 1, mod3.shape[2]), lambda i, j: (i, 0, 0)),
        _const_spec((1, d)),
        _const_spec(w["win"].shape), _const_spec(w["wkT"].shape),
        _const_spec(w["wa2"].shape), _const_spec((1, HK)),
        _const_spec(w["wa2T"].shape), _const_spec((HK, 1)),
        _const_spec((1, Q_LORA)), _const_spec((1, KV_LORA)),
        _const_spec(w["wckvT"].shape), _const_spec((KV_LORA, 1)),
        _const_spec(w["wuq"].shape), _const_spec(w["wukT"].shape),
        pl.BlockSpec((tt, 2 * QK_ROPE), lambda i, j: (j, 0)),
    ]
    return pl.pallas_call(
        _inproj_kernel, out_shape=out_shape, grid=grid, in_specs=in_specs, out_specs=out_specs,
        compiler_params=_cparams(("arbitrary", "arbitrary")), name="inproj",
    )(x, mod3, w["g_pre_mix"], w["win"], w["wkT"], w["wa2"], w["ba"], w["wa2T"], w["baT"],
      w["g_mla_q"], w["g_mla_kv"], w["wckvT"], w["g_mla_kvT"], w["wuq"], w["wukT"], cs)


def _gla_kernel(q_ref, k_ref, la_ref, kT_ref, laT_ref, v_ref, r_ref, s0_ref, g_ref,
                o_ref, sfin_ref, s_ref, *, chunk, sub):
    j = pl.program_id(1)
    c_, r_ = chunk, sub
    nsub = c_ // r_
    tc = q_ref.shape[1]

    @pl.when(j == 0)
    def _():
        s_ref[...] = s0_ref[0]

    row = lax.broadcasted_iota(jnp.int32, (2 * c_, c_), 0)
    col = lax.broadcasted_iota(jnp.int32, (2 * c_, c_), 1)
    bound = jnp.where(row < c_, row + 1, ((row - c_) // r_) * r_)
    lmat = jnp.where(col < bound, 1.0, 0.0).astype(BF16)
    ur = lax.broadcasted_iota(jnp.int32, (c_, c_), 0)
    uc = lax.broadcasted_iota(jnp.int32, (c_, c_), 1)
    umat = jnp.where(ur <= uc, 1.0, 0.0).astype(BF16)
    lane = lax.broadcasted_iota(jnp.int32, (1, HK), 1)
    hmask = [(lane >= hh * DK_A) & (lane < (hh + 1) * DK_A) for hh in range(H_A)]
    g_out = g_ref[...]

    def stack_heads(a):
        return jnp.concatenate([jnp.where(hmask[hh], a, 0.0) for hh in range(H_A)], axis=0).astype(BF16)

    for c in range(tc // c_):
        sl = slice(c * c_, (c + 1) * c_)
        g1, g2, g3 = _split3(la_ref[0, sl, :])
        bm = _mm(lmat, g1) + _mm(lmat, g2) + _mm(lmat, g3)
        b, mrow = bm[:c_], bm[c_:]
        t1, t2, t3 = _split3(laT_ref[0, :, sl])
        bT = _mm(t1, umat) + _mm(t2, umat) + _mm(t3, umat)
        q = q_ref[0, sl, :]
        k = k_ref[0, sl, :]
        v = v_ref[0, sl, :].astype(BF16)
        s_old = s_ref[...]

        o_inter = _mm(stack_heads(q * jnp.exp(b)), s_old.astype(BF16))

        qt = q * jnp.exp(b - mrow)
        o_sub = [[] for _ in range(H_A)]
        for i in range(nsub):
            n_k = (i + 1) * r_
            m_i = mrow[i * r_:i * r_ + 1, :]
            kt = (k[:n_k] * jnp.exp(jnp.minimum(m_i - b[:n_k], GLA_EXP_CLAMP))).astype(BF16)
            att = _mm_nt(stack_heads(qt[i * r_:(i + 1) * r_]), kt)
            rr = lax.broadcasted_iota(jnp.int32, (H_A * r_, n_k), 0) % r_
            cc = lax.broadcasted_iota(jnp.int32, (H_A * r_, n_k), 1)
            att = jnp.where(cc <= rr + i * r_, att, 0.0).astype(BF16)
            pv = _mm(att, v[:n_k])
            for hh in range(H_A):
                o_sub[hh].append(pv[hh * r_:(hh + 1) * r_, hh * DV_A:(hh + 1) * DV_A])
        for hh in range(H_A):
            o_h = o_inter[hh * c_:(hh + 1) * c_] + jnp.concatenate(o_sub[hh], axis=0)
            gate = r_ref[0, sl, hh * DV_A:(hh + 1) * DV_A].astype(F32)
            o_ref[0, sl, hh * DV_A:(hh + 1) * DV_A] = (_rms(o_h, g_out) * _silu(gate)).astype(o_ref.dtype)

        bl = bT[:, c_ - 1:c_]
        klT = (kT_ref[0, :, sl] * jnp.exp(bl - bT)).astype(BF16)
        upd = _mm(klT, v)
        decay = jnp.exp(bl)
        s_ref[...] = jnp.concatenate(
            [s_old[hh * DK_A:(hh + 1) * DK_A] * decay[hh * DK_A:(hh + 1) * DK_A]
             + upd[hh * DK_A:(hh + 1) * DK_A, hh * DV_A:(hh + 1) * DV_A] for hh in range(H_A)], axis=0)

    @pl.when(j == pl.num_programs(1) - 1)
    def _():
        sfin_ref[0] = s_ref[...]


def _gla(qa, ka, la, kT, laT, va, ra, s0, g_out, *, chunk, sub, tc, out_dtype):
    nb, t, _ = qa.shape
    grid = (nb, t // tc)
    tok = lambda width: pl.BlockSpec((1, tc, width), lambda b, j: (b, j, 0))
    tsp = pl.BlockSpec((1, HK, tc), lambda b, j: (b, 0, j))
    st = pl.BlockSpec((1, HK, DV_A), lambda b, j: (b, 0, 0))
    return pl.pallas_call(
        functools.partial(_gla_kernel, chunk=chunk, sub=sub),
        out_shape=(jax.ShapeDtypeStruct((nb, t, HV), out_dtype),
                   jax.ShapeDtypeStruct((nb, HK, DV_A), F32)),
        grid=grid,
        in_specs=[tok(HK), tok(HK), tok(HK), tsp, tsp, tok(HV), tok(HV), st, _const_spec((1, DV_A))],
        out_specs=(tok(HV), st),
        scratch_shapes=[pltpu.VMEM((HK, DV_A), F32)],
        compiler_params=_cparams(("arbitrary", "arbitrary")), name="gla",
    )(qa, ka, la, kT, laT, va, ra, s0, g_out)


def _flash_kernel(q_ref, k_ref, vT_ref, wuvT_ref, g_ref, o_ref, m_ref, l_ref, acc_ref):
    qi = pl.program_id(1)
    tq = q_ref.shape[2]
    rows = H_B * tq
    q = q_ref[0].reshape(rows, QCAT)
    m_ref[...] = jnp.full(m_ref.shape, -jnp.inf, F32)
    l_ref[...] = jnp.zeros(l_ref.shape, F32)
    acc_ref[...] = jnp.zeros(acc_ref.shape, F32)

    def scores(j):
        kc = k_ref[0, pl.ds(pl.multiple_of(j * tq, tq), tq), :]
        return _mm_nt(kc, q)

    def update(j, st, diagonal):
        if diagonal:
            t_k = lax.broadcasted_iota(jnp.int32, (tq, rows), 0)
            t_q = lax.broadcasted_iota(jnp.int32, (tq, rows), 1) % tq
            st = jnp.where(t_k <= t_q, st, -jnp.inf)
        m_prev = m_ref[...]
        m_new = jnp.maximum(m_prev, jnp.max(st, axis=0, keepdims=True))
        alpha = jnp.exp2(m_prev - m_new)
        p = jnp.exp2(st - m_new)
        l_ref[...] = alpha * l_ref[...] + jnp.sum(p, axis=0, keepdims=True)
        acc_ref[...] = alpha * acc_ref[...] + _mm(vT_ref[0, j], p.astype(BF16))
        m_ref[...] = m_new

    def full_block(j, st):
        st_next = scores(j + 1)
        update(j, st, False)
        return st_next

    update(qi, lax.fori_loop(0, qi, full_block, scores(0)), True)

    o_lat_t = (acc_ref[...] / l_ref[...]).astype(BF16)
    for hh in range(H_B):
        ob_t = _mm(wuvT_ref[hh], o_lat_t[:, hh * tq:(hh + 1) * tq])
        y_t = ob_t * lax.rsqrt(jnp.mean(ob_t * ob_t, axis=0, keepdims=True) + EPS) * g_ref[...]
        o_ref[0, :, hh * V_B:(hh + 1) * V_B] = y_t.T.astype(o_ref.dtype)


def _flash(qcat, kcat, lat_t, wuv_t, g_col, *, tq):
    nb, _, t, _ = qcat.shape
    rows = H_B * tq
    return pl.pallas_call(
        _flash_kernel,
        out_shape=jax.ShapeDtypeStruct((nb, t, H_B * V_B), BF16),
        grid=(nb, t // tq),
        in_specs=[pl.BlockSpec((1, H_B, tq, QCAT), lambda b, i: (b, 0, i, 0)),
                  pl.BlockSpec((1, t, QCAT), lambda b, i: (b, 0, 0)),
                  pl.BlockSpec((1, t // tq, KV_LORA, tq), lambda b, i: (b, 0, 0, 0)),
                  _const_spec((H_B, V_B, KV_LORA)), _const_spec((V_B, 1))],
        out_specs=pl.BlockSpec((1, tq, H_B * V_B), lambda b, i: (b, i, 0)),
        scratch_shapes=[pltpu.VMEM((1, rows), F32), pltpu.VMEM((1, rows), F32),
                        pltpu.VMEM((KV_LORA, rows), F32)],
        compiler_params=_cparams(("arbitrary", "arbitrary")), name="flash",
    )(qcat, kcat, lat_t, wuv_t, g_col)


def _decode_kernel(pt_ref, q_ref, knew_ref, wuv_ref, g_ref, lat_hbm, krT_hbm, o_ref,
                   lat_buf, kr_buf, s_scr, p_scr, sems, *, n_pages, td, ck):
    b = pl.program_id(0)
    slot = b % 2
    page = lat_hbm.shape[1]
    nck = (n_pages * page) // ck
    ppc = ck // page

    def page_copies(seq, sl, p):
        pg = pt_ref[seq * n_pages + p]
        rows = pl.ds(pl.multiple_of(p * page, page), page)
        return (pltpu.make_async_copy(lat_hbm.at[pg], lat_buf.at[sl, rows, :], sems.at[0, sl]),
                pltpu.make_async_copy(krT_hbm.at[pg], kr_buf.at[sl, p], sems.at[1, sl]))

    def fetch(seq, sl):
        def body(p, carry):
            for cp in page_copies(seq, sl, p):
                cp.start()
            return carry
        lax.fori_loop(0, n_pages, body, 0, unroll=8)

    @pl.when(b == 0)
    def _():
        fetch(0, 0)

    @pl.when(b + 1 < pl.num_programs(0))
    def _():
        fetch(b + 1, 1 - slot)

    pltpu.make_async_copy(lat_buf.at[slot], lat_buf.at[slot], sems.at[0, slot]).wait()
    pltpu.make_async_copy(kr_buf.at[slot], kr_buf.at[slot], sems.at[1, slot]).wait()

    q = q_ref[0].astype(BF16)
    ql, qr = q[:, 0:KV_LORA], q[:, KV_LORA:KV_LORA + QK_ROPE]

    def lat_chunk(c):
        return lat_buf[slot, c * ck:(c + 1) * ck, :].astype(BF16)

    for c in range(nck):
        s_rope = jnp.concatenate(
            [_mm(qr, kr_buf[slot, c * ppc + i].astype(BF16)) for i in range(ppc)], axis=1)
        s_scr[c] = _mm_nt(ql, lat_chunk(c)) + s_rope

    kn = knew_ref[0].astype(BF16)
    s_new = _mm_nt(q, kn)
    t_q = lax.broadcasted_iota(jnp.int32, s_new.shape, 0) % td
    t_k = lax.broadcasted_iota(jnp.int32, s_new.shape, 1)
    s_new = jnp.where(t_k <= t_q, s_new, -jnp.inf)
    s_all = s_scr[...]
    m = jnp.maximum(jnp.max(jnp.max(s_all, axis=0), axis=-1, keepdims=True),
                    jnp.max(s_new, axis=-1, keepdims=True))
    p_all = jnp.exp2(s_all - m[None])
    p_new = jnp.exp2(s_new - m)
    l = jnp.sum(jnp.sum(p_all, axis=0), axis=-1, keepdims=True) + jnp.sum(p_new, axis=-1, keepdims=True)
    p_scr[...] = p_all.astype(BF16)

    parts = [_mm(p_new.astype(BF16), kn[:, 0:KV_LORA])] + [_mm(p_scr[c], lat_chunk(c)) for c in range(nck)]
    while len(parts) > 1:
        parts = [parts[i] + parts[i + 1] if i + 1 < len(parts) else parts[i] for i in range(0, len(parts), 2)]
    o_lat = (parts[0] / l).astype(BF16)
    ob = _mm(o_lat, wuv_ref[...])
    for hh in range(H_B):
        blk = ob[hh * td:(hh + 1) * td, hh * V_B:(hh + 1) * V_B]
        o_ref[0, :, hh * V_B:(hh + 1) * V_B] = _rms(blk, g_ref[...]).astype(o_ref.dtype)


def _decode(qcat, knew, wuv_all, g_out, cache_lat, cache_krT, page_table, *, ck):
    nb, _, td, _ = qcat.shape
    qcat = qcat.reshape(nb, H_B * td, QCAT)
    n_pages = page_table.shape[1]
    page = cache_lat.shape[1]
    past = n_pages * page
    rows = H_B * td
    grid_spec = pltpu.PrefetchScalarGridSpec(
        num_scalar_prefetch=1,
        grid=(nb,),
        in_specs=[pl.BlockSpec((1, rows, QCAT), lambda b, pt: (b, 0, 0)),
                  pl.BlockSpec((1, LANE, QCAT), lambda b, pt: (b, 0, 0)),
                  pl.BlockSpec(wuv_all.shape, lambda b, pt: (0, 0)),
                  pl.BlockSpec((1, V_B), lambda b, pt: (0, 0)),
                  pl.BlockSpec(memory_space=pl.ANY),
                  pl.BlockSpec(memory_space=pl.ANY)],
        out_specs=pl.BlockSpec((1, td, H_B * V_B), lambda b, pt: (b, 0, 0)),
        scratch_shapes=[pltpu.VMEM((2, past, KV_LORA), F32),
                        pltpu.VMEM((2, n_pages, QK_ROPE, page), F32),
                        pltpu.VMEM((past // ck, rows, ck), F32),
                        pltpu.VMEM((past // ck, rows, ck), BF16),
                        pltpu.SemaphoreType.DMA((2, 2))],
    )
    return pl.pallas_call(
        functools.partial(_decode_kernel, n_pages=n_pages, td=td, ck=ck),
        out_shape=jax.ShapeDtypeStruct((nb, td, H_B * V_B), F32),
        grid_spec=grid_spec,
        compiler_params=_cparams(("arbitrary",)), name="decode",
    )(page_table.reshape(-1), qcat, knew, wuv_all, g_out, cache_lat, cache_krT)


def _outproj_kernel(oa_ref, ob_ref, x_ref, mod_ref, wo_ref, gpost_ref, gpre_ref, x1_ref, h2_ref):
    bb, tt, d = x_ref.shape
    m = bb * tt
    mod = mod_ref[...]
    gt1, sh2, sc2 = mod[:, :, 2 * d:3 * d], mod[:, :, 3 * d:4 * d], mod[:, :, 4 * d:5 * d]
    mix = (_mm(oa_ref[...].reshape(m, HV).astype(BF16), wo_ref[0:HV, :])
           + _mm(ob_ref[...].reshape(m, H_B * V_B).astype(BF16), wo_ref[HV:HV + H_B * V_B, :]))
    x1 = x_ref[...] + gt1 * _rms(mix, gpost_ref[...]).reshape(bb, tt, d)
    x1_ref[...] = x1
    h2_ref[...] = (_rms(x1, gpre_ref[...]) * (1.0 + sc2) + sh2).astype(h2_ref.dtype)


def _outproj(oa, ob, x, mod3, w, *, bb, tt, act_dtype):
    nb, t, d = x.shape
    grid = (nb // bb, t // tt)
    tok = lambda width: pl.BlockSpec((bb, tt, width), lambda i, j: (i, j, 0))
    return pl.pallas_call(
        _outproj_kernel,
        out_shape=(jax.ShapeDtypeStruct((nb, t, d), F32), jax.ShapeDtypeStruct((nb, t, d), act_dtype)),
        grid=grid,
        in_specs=[tok(HV), tok(H_B * V_B), tok(d),
                  pl.BlockSpec((bb, 1, mod3.shape[2]), lambda i, j: (i, 0, 0)),
                  _const_spec(w["wo"].shape), _const_spec((1, d)), _const_spec((1, d))],
        out_specs=(tok(d), tok(d)),
        compiler_params=_cparams(("arbitrary", "arbitrary")), name="outproj",
    )(oa, ob, x, mod3, w["wo"], w["g_post_mix"], w["g_pre_ffn"])


def _gelu_tanh(x):
    c = math.sqrt(2.0 / math.pi)
    hx = 0.5 * x
    return hx + hx * jnp.tanh(x * (c + (c * 0.044715) * (x * x)))


def _ffn_kernel(h2_ref, x1_ref, mod_ref, cst_ref, wup_ref, wc_ref, bc_ref, wdn_ref, gpost_ref,
                y_ref, cnew_ref, carry_ref):
    bb, tt, d = x1_ref.shape
    m = bb * tt
    nc, fc, _ = wdn_ref.shape

    @pl.when(pl.program_id(1) == 0)
    def _():
        carry_ref[...] = cst_ref[...]

    assert bb == 1 or tt == SUBLANE
    ng = m // SUBLANE
    h2 = h2_ref[...].reshape(m, d).astype(BF16)
    sub = lax.broadcasted_iota(jnp.int32, (1, SUBLANE, fc), 1)

    def up_conv(k):
        cols = slice(k * fc, (k + 1) * fc)
        u = _mm(h2, wup_ref[k])
        prev = carry_ref[:, :, cols]
        tail = u.reshape(bb, tt, fc)[:, tt - (CONV_W - 1):tt]
        carry_ref[:, :, cols] = tail
        cnew_ref[:, :, cols] = tail
        g3 = u.reshape(ng, SUBLANE, fc)
        r1, r2 = pltpu.roll(g3, 1, axis=1), pltpu.roll(g3, 2, axis=1)
        p1 = jnp.broadcast_to(prev[:, 1:2], (bb, SUBLANE, fc))
        p2 = jnp.where(sub == 0, jnp.broadcast_to(prev[:, 0:1], (bb, SUBLANE, fc)), p1)
        if bb == 1:
            p1 = jnp.concatenate([p1, r1[:ng - 1]], axis=0)
            p2 = jnp.concatenate([p2, r2[:ng - 1]], axis=0)
        u1 = jnp.where(sub < 1, p1, r1).reshape(m, fc)
        u2 = jnp.where(sub < 2, p2, r2).reshape(m, fc)
        wc = wc_ref[k]
        return bc_ref[k] + wc[2:3] * u + wc[1:2] * u1 + wc[0:1] * u2

    acc = jnp.zeros((m, d), F32)
    ready = [(up_conv(c), up_conv(nc + c)) for c in range(min(FFN_AHEAD, nc))]
    for c in range(nc):
        if c + FFN_AHEAD < nc:
            ready.append((up_conv(c + FFN_AHEAD), up_conv(nc + c + FFN_AHEAD)))
        val, gate = ready.pop(0)
        act = (val * _gelu_tanh(gate)).astype(BF16)
        acc = acc + _mm(act, wdn_ref[c])
    gt2 = mod_ref[...][:, :, 5 * d:6 * d]
    y_ref[...] = x1_ref[...] + gt2 * _rms(acc, gpost_ref[...]).reshape(bb, tt, d)


def _ffn(h2, x1, mod3, cstate, w, *, bb, tt):
    nb, t, d = x1.shape
    f2 = cstate.shape[2]
    grid = (nb // bb, t // tt)
    tok = lambda width: pl.BlockSpec((bb, tt, width), lambda i, j: (i, j, 0))
    cs_spec = pl.BlockSpec((bb, CONV_W - 1, f2), lambda i, j: (i, 0, 0))
    return pl.pallas_call(
        _ffn_kernel,
        out_shape=(jax.ShapeDtypeStruct((nb, t, d), F32),
                   jax.ShapeDtypeStruct((nb, CONV_W - 1, f2), F32)),
        grid=grid,
        in_specs=[tok(d), tok(d),
                  pl.BlockSpec((bb, 1, mod3.shape[2]), lambda i, j: (i, 0, 0)),
                  cs_spec,
                  _const_spec(w["wup"].shape), _const_spec(w["wconv"].shape), _const_spec(w["bconv"].shape),
                  _const_spec(w["wdn"].shape), _const_spec((1, d))],
        out_specs=(tok(d), cs_spec),
        scratch_shapes=[pltpu.VMEM((bb, CONV_W - 1, f2), F32)],
        compiler_params=_cparams(("arbitrary", "arbitrary")), name="ffn",
    )(h2, x1, mod3, cstate, w["wup"], w["wconv"], w["bconv"], w["wdn"], w["g_post_ffn"])


def _rot_cols(wr):
    half = wr.shape[-1] // 2
    return jnp.concatenate([-wr[..., half:], wr[..., :half]], axis=-1)


def _prep_weights(l, w_in, w_gla_a2, b_gla_a, g_gla_out, g_mla_q, g_mla_kv, w_mla_uq, w_mla_uk, w_mla_uv,
                  g_mla_out, w_o, w_ffn_up, w_ffn_conv, b_ffn_conv, w_ffn_down, g_pre_mix, g_post_mix,
                  g_pre_ffn, g_post_ffn):
    wi = w_in[l]
    d = wi.shape[0]
    o_gr = 2 * HK + 2 * HV
    o_cq = o_gr + GATE_RANK
    o_ckv = o_cq + Q_LORA
    o_kr = o_ckv + KV_LORA
    kr = wi[:, o_kr:o_kr + QK_ROPE]
    win = jnp.concatenate(
        [wi[:, 0:o_gr], wi[:, o_cq:o_ckv], wi[:, o_ckv:o_kr], kr, _rot_cols(kr),
         wi[:, o_gr:o_cq], jnp.zeros((d, LANE - GATE_RANK), wi.dtype)], axis=1).astype(BF16)
    wa2 = jnp.concatenate([w_gla_a2[l], jnp.zeros((LANE - GATE_RANK, HK), F32)], axis=0).astype(BF16)
    uq = w_mla_uq[l].reshape(Q_LORA, H_B, QK_NOPE + QK_ROPE)
    uq_rope = uq[:, :, QK_NOPE:]
    wuq = jnp.concatenate(
        [uq[:, :, :QK_NOPE].reshape(Q_LORA, H_B * QK_NOPE),
         jnp.concatenate([uq_rope, _rot_cols(uq_rope)], axis=-1).reshape(Q_LORA, H_B * 2 * QK_ROPE)],
        axis=1).astype(BF16)
    return dict(
        win=win, wkT=wi[:, HK:2 * HK].T.astype(BF16),
        wa2=wa2, ba=b_gla_a[l].reshape(1, HK), wa2T=wa2.T, baT=b_gla_a[l].reshape(HK, 1),
        g_gla_out=g_gla_out[l].reshape(1, DV_A),
        g_mla_q=g_mla_q[l].reshape(1, Q_LORA), g_mla_kv=g_mla_kv[l].reshape(1, KV_LORA),
        wuq=wuq, wukT=jnp.transpose(w_mla_uk[l], (1, 2, 0)).astype(BF16),
        wckvT=wi[:, o_ckv:o_kr].T.astype(BF16), g_mla_kvT=g_mla_kv[l].reshape(KV_LORA, 1),
        wuvT=jnp.transpose(w_mla_uv[l], (1, 2, 0)).astype(BF16),
        g_mla_outT=g_mla_out[l].reshape(V_B, 1),
        wuv_all=w_mla_uv[l].reshape(KV_LORA, H_B * V_B).astype(BF16),
        g_mla_out=g_mla_out[l].reshape(1, V_B),
        wo=w_o[l].astype(BF16),
        wup=w_ffn_up[l].astype(BF16).reshape(d, -1, FFN_CHUNK).transpose(1, 0, 2),
        wconv=w_ffn_conv[l].reshape(CONV_W, -1, FFN_CHUNK).transpose(1, 0, 2),
        bconv=b_ffn_conv[l].reshape(-1, 1, FFN_CHUNK),
        wdn=w_ffn_down[l].astype(BF16).reshape(-1, FFN_CHUNK, d),
        g_pre_mix=g_pre_mix[l].reshape(1, d), g_post_mix=g_post_mix[l].reshape(1, d),
        g_pre_ffn=g_pre_ffn[l].reshape(1, d), g_post_ffn=g_post_ffn[l].reshape(1, d),
    )


def _rope_table(pos):
    inv = ROPE_BASE ** (-jnp.arange(0, QK_ROPE, 2, dtype=F32) / QK_ROPE)
    ang = pos.astype(F32)[:, None] * inv[None, :]
    cos, sin = jnp.cos(ang), jnp.sin(ang)
    return jnp.concatenate([cos, cos, sin, sin], axis=-1)


def _token_tile(t, target):
    tt = min(t, target)
    while t % tt:
        tt //= 2
    return tt


def _layer(x, mod3, cs, w, gla_s0, conv_state, *, per_seq, attend):
    nb, t, d = x.shape
    if per_seq:
        bb, tt = nb, t
    else:
        bb, tt = 1, _token_tile(t, 512)
    act_dtype = F32 if per_seq else BF16
    tkb = _token_tile(bb * tt, ATTN_TILE)
    qa, ka, va, ra, la, kT, laT, qcat, kcat, lat_t, lat, kr = _inproj(
        x, mod3, cs, w, bb=bb, tt=tt, tkb=tkb, act_dtype=act_dtype)

    if per_seq:
        tp = 2 * SUBLANE
        padt = lambda a: jnp.pad(a, ((0, 0), (0, tp - t), (0, 0)))
        tr = lambda a: jnp.pad(a.reshape(HK, nb, t).transpose(1, 0, 2), ((0, 0), (0, 0), (0, tp - t)))
        o_a, s_fin = _gla(padt(qa), padt(ka), padt(la), tr(kT), tr(laT), padt(va), padt(ra), gla_s0,
                          w["g_gla_out"], chunk=tp, sub=tp, tc=tp, out_dtype=act_dtype)
        o_a = o_a[:, :t]
    else:
        o_a, s_fin = _gla(qa, ka, la, kT, laT, va, ra, gla_s0, w["g_gla_out"],
                          chunk=LANE, sub=LANE // 4, tc=_token_tile(t, 256), out_dtype=act_dtype)

    o_b = attend(qcat, kcat, lat_t)
    x1, h2 = _outproj(o_a, o_b, x, mod3, w, bb=bb, tt=tt, act_dtype=act_dtype)
    ffn_tt = tt if per_seq else _token_tile(t, FFN_ROWS)
    y, c_new = _ffn(h2, x1, mod3, conv_state, w, bb=bb, tt=ffn_tt)
    return y, lat, kr, s_fin, c_new


def kernel(x_prompt, x_sample, c_prompt, c_sample, cache_latent, cache_krope, state_gla, state_conv, page_table, w_ada, b_ada, g_pre_mix, g_post_mix, g_pre_ffn, g_post_ffn, w_in, w_gla_a2, b_gla_a, g_gla_out, g_mla_q, g_mla_kv, w_mla_uq, w_mla_uk, w_mla_uv, g_mla_out, w_o, w_ffn_up, w_ffn_conv, b_ffn_conv, w_ffn_down):
    depth = w_ada.shape[0]
    nbp, tp, d = x_prompt.shape
    nbs, ts, _ = x_sample.shape
    n_pages, page = page_table.shape[1], cache_latent.shape[2]
    past_len = n_pages * page
    f2 = w_ffn_up.shape[2]
    cs_p = _rope_table(jnp.arange(tp, dtype=jnp.int32))
    cs_s = _rope_table(past_len + jnp.arange(ts, dtype=jnp.int32))

    hp, hs = x_prompt, x_sample
    outs = [[] for _ in range(8)]
    for l in range(depth):
        w = _prep_weights(l, w_in, w_gla_a2, b_gla_a, g_gla_out, g_mla_q, g_mla_kv, w_mla_uq, w_mla_uk,
                          w_mla_uv, g_mla_out, w_o, w_ffn_up, w_ffn_conv, b_ffn_conv, w_ffn_down,
                          g_pre_mix, g_post_mix, g_pre_ffn, g_post_ffn)
        mod = _ada(jnp.concatenate([c_prompt, c_sample], axis=0), w_ada[l], b_ada[l])
        mod3 = mod.reshape(nbp + nbs, 1, mod.shape[1])

        def attend_prompt(qcat, kcat, lat_t):
            return _flash(qcat, kcat, lat_t, w["wuvT"], w["g_mla_outT"], tq=_token_tile(tp, ATTN_TILE))

        def attend_sample(qcat, kcat, lat_t):
            knew = jnp.pad(kcat, ((0, 0), (0, LANE - ts), (0, 0)))
            return _decode(qcat, knew, w["wuv_all"], w["g_mla_out"], cache_latent[l],
                           jnp.swapaxes(cache_krope[l], 1, 2), page_table, ck=8 * page)

        hp, a1, a2, a3, a4 = _layer(
            hp, mod3[:nbp], cs_p, w, jnp.zeros((nbp, HK, DV_A), F32),
            jnp.zeros((nbp, CONV_W - 1, f2), F32), per_seq=False, attend=attend_prompt)
        hs, b1, b2, b3, b4 = _layer(
            hs, mod3[nbp:], cs_s, w, state_gla[l].reshape(nbs, HK, DV_A), state_conv[l],
            per_seq=True, attend=attend_sample)
        for lst, val in zip(outs, (a1, a2, a3.reshape(nbp, H_A, DK_A, DV_A), a4,
                                   b1, b2, b3.reshape(nbs, H_A, DK_A, DV_A), b4)):
            lst.append(val)
    return (hp, hs) + tuple(jnp.stack(o) for o in outs)
```

```python
import functools
import math

import jax
import jax.numpy as jnp
import numpy as np
from jax import lax
from jax.experimental import pallas as pl
from jax.experimental.pallas import tpu as pltpu

F32 = jnp.float32
BF16 = jnp.bfloat16

EPS = 1e-6
H_A, DK_A, DV_A = 4, 64, 128
GATE_RANK = 16
GATE_TAU = 16.0
H_B = 4
Q_LORA, KV_LORA = 384, 256
QK_NOPE, QK_ROPE, V_B = 128, 64, 128
ROPE_BASE = 10000.0
CONV_W = 3

HK = H_A * DK_A
HV = H_A * DV_A
QCAT = KV_LORA + 2 * QK_ROPE
LANE = 128
SUBLANE = 8

_O_QA, _O_KA, _O_VA, _O_RA = 0, HK, 2 * HK, 2 * HK + HV
_O_CQ = 2 * HK + 2 * HV
_O_CKV = _O_CQ + Q_LORA
_O_KRR = _O_CKV + KV_LORA
_O_GR = _O_KRR + 2 * QK_ROPE
IN_EXT = _O_GR + LANE

QK_SCALE_LOG2E = (QK_NOPE + QK_ROPE) ** -0.5 * math.log2(math.e)

GLA_CHUNK_SHORT = 16
ATTN_TILE = 256
FFN_CHUNK = 256
FFN_ROWS = 256
FFN_AHEAD = 4
VMEM_LIMIT = 56 * 1024 * 1024
GLA_EXP_CLAMP = 80.0


def _mm(a, b):
    return jnp.dot(a, b, preferred_element_type=F32)


def _mm_nt(a, b):
    return lax.dot_general(a, b, (((1,), (1,)), ((), ())), preferred_element_type=F32)


def _rms(x, g):
    return x * lax.rsqrt(jnp.mean(x * x, axis=-1, keepdims=True) + EPS) * g


def _silu(x):
    return x / (1.0 + jnp.exp(-x))


def _log_sigmoid(x):
    return -(jnp.maximum(-x, 0.0) + jnp.log(1.0 + jnp.exp(-jnp.abs(x))))


def _cparams(sem):
    return pltpu.CompilerParams(dimension_semantics=sem, vmem_limit_bytes=VMEM_LIMIT)


def _const_spec(shape):
    nd = len(shape)
    return pl.BlockSpec(shape, lambda *_: (0,) * nd)


def _ada_kernel(c_ref, w_ref, b_ref, o_ref):
    s = _silu(c_ref[...]).astype(BF16)
    o_ref[...] = _mm(s, w_ref[...].astype(BF16)) + b_ref[...]


def _ada(c_all, w_ada, b_ada):
    n, d = c_all.shape
    nout = w_ada.shape[1]
    tn = d
    return pl.pallas_call(
        _ada_kernel,
        out_shape=jax.ShapeDtypeStruct((n, nout), F32),
        grid=(nout // tn,),
        in_specs=[pl.BlockSpec((n, d), lambda j: (0, 0)),
                  pl.BlockSpec((d, tn), lambda j: (0, j)),
                  pl.BlockSpec((1, tn), lambda j: (0, j))],
        out_specs=pl.BlockSpec((n, tn), lambda j: (0, j)),
        compiler_params=_cparams(("arbitrary",)),
        name="ada",
    )(c_all, w_ada, b_ada.reshape(1, nout))


def _rope_sum(a3, cs):
    bb, tt, w = a3.shape
    a = (a3 * cs[None]).reshape(bb * tt, w)
    return a + pltpu.roll(a, QK_ROPE, axis=1)


def _inproj_kernel(x_ref, mod_ref, gpre_ref, win_ref, wa2_ref, ba_ref,
                   gq_ref, gkv_ref, wckvT_ref, gkvT_ref, wuq_ref, wukT_ref, cs_ref,
                   qa_ref, ka_ref, va_ref, ra_ref, la_ref,
                   qcat_ref, kcat_ref, latT_ref, lat_ref, kr_ref):
    bb, tt, d = x_ref.shape
    m = bb * tt
    x = x_ref[...]
    mod = mod_ref[...]
    sh1, sc1 = mod[:, :, 0:d], mod[:, :, d:2 * d]
    h = _rms(x, gpre_ref[...]) * (1.0 + sc1) + sh1
    h = h.reshape(m, d).astype(BF16)
    z = _mm(h, win_ref[...])

    qa_ref[...] = (z[:, _O_QA:_O_QA + HK] * (DK_A ** -0.5)).reshape(bb, tt, HK)
    ka_ref[...] = z[:, _O_KA:_O_KA + HK].reshape(bb, tt, HK)
    va_ref[...] = z[:, _O_VA:_O_VA + HV].astype(va_ref.dtype).reshape(bb, tt, HV)
    ra_ref[...] = z[:, _O_RA:_O_RA + HV].astype(ra_ref.dtype).reshape(bb, tt, HV)
    gr = z[:, _O_GR:_O_GR + LANE].astype(BF16)
    la = _log_sigmoid(_mm(gr, wa2_ref[...]) + ba_ref[...]) * (1.0 / GATE_TAU)
    la_ref[...] = la.reshape(bb, tt, HK)

    cs = cs_ref[...]
    ckv = _rms(z[:, _O_CKV:_O_CKV + KV_LORA], gkv_ref[...])
    lat_ref[...] = ckv.reshape(bb, tt, KV_LORA)
    krs = _rope_sum(z[:, _O_KRR:_O_KRR + 2 * QK_ROPE].reshape(bb, tt, 2 * QK_ROPE), cs)
    if len(kr_ref.shape) == 2:
        kr_ref[...] = krs.T[0:QK_ROPE, :]
    else:
        kr_ref[...] = krs[:, 0:QK_ROPE].reshape(bb, tt, QK_ROPE)
    ckv_t = _mm_nt(wckvT_ref[...], h)
    ckv_t = ckv_t * lax.rsqrt(jnp.mean(ckv_t * ckv_t, axis=0, keepdims=True) + EPS) * gkvT_ref[...]
    tkb = latT_ref.shape[2]
    for c in range(latT_ref.shape[0]):
        latT_ref[c] = ckv_t[:, c * tkb:(c + 1) * tkb].astype(latT_ref.dtype)
    lane = lax.broadcasted_iota(jnp.int32, (m, 2 * QK_ROPE), 1)
    kcat_ref[:, :, 0:KV_LORA] = ckv.astype(kcat_ref.dtype).reshape(bb, tt, KV_LORA)
    kcat_ref[:, :, KV_LORA:QCAT] = (
        jnp.where(lane < QK_ROPE, krs, 0.0).astype(kcat_ref.dtype).reshape(bb, tt, 2 * QK_ROPE))

    cq = _rms(z[:, _O_CQ:_O_CQ + Q_LORA], gq_ref[...]).astype(BF16)
    qb = _mm(cq, wuq_ref[...])
    for hh in range(H_B):
        qn = qb[:, hh * QK_NOPE:(hh + 1) * QK_NOPE].astype(BF16)
        qlat = _mm(qn, wukT_ref[hh])
        qcat_ref[:, hh, :, 0:KV_LORA] = (qlat * QK_SCALE_LOG2E).astype(qcat_ref.dtype).reshape(bb, tt, KV_LORA)
        o = H_B * QK_NOPE + hh * 2 * QK_ROPE
        qrs = _rope_sum(qb[:, o:o + 2 * QK_ROPE].reshape(bb, tt, 2 * QK_ROPE), cs) * QK_SCALE_LOG2E
        qcat_ref[:, hh, :, KV_LORA:QCAT] = qrs.astype(qcat_ref.dtype).reshape(bb, tt, 2 * QK_ROPE)


def _inproj(x, mod3, cs, w, *, bb, tt, tkb, act_dtype):
    nb, t, d = x.shape
    m = bb * tt
    grid = (nb // bb, t // tt)
    tok = lambda width: pl.BlockSpec((bb, tt, width), lambda i, j: (i, j, 0))
    out_shape = (
        jax.ShapeDtypeStruct((nb, t, HK), F32),
        jax.ShapeDtypeStruct((nb, t, HK), F32),
        jax.ShapeDtypeStruct((nb, t, HV), act_dtype),
        jax.ShapeDtypeStruct((nb, t, HV), act_dtype),
        jax.ShapeDtypeStruct((nb, t, HK), F32),
        jax.ShapeDtypeStruct((nb, H_B, t, QCAT), act_dtype),
        jax.ShapeDtypeStruct((nb, t, QCAT), act_dtype),
        jax.ShapeDtypeStruct((nb // bb, (t // tt) * (m // tkb), KV_LORA, tkb), BF16),
        jax.ShapeDtypeStruct((nb, t, KV_LORA), F32),
        jax.ShapeDtypeStruct((nb, QK_ROPE, t) if bb == 1 else (nb, t, QK_ROPE), F32),
    )
    kr_spec = pl.BlockSpec((None, QK_ROPE, tt), lambda i, j: (i, 0, j)) if bb == 1 else tok(QK_ROPE)
    out_specs = (tok(HK), tok(HK), tok(HV), tok(HV), tok(HK),
                 pl.BlockSpec((bb, H_B, tt, QCAT), lambda i, j: (i, 0, j, 0)),
                 tok(QCAT),
                 pl.BlockSpec((None, m // tkb, KV_LORA, tkb), lambda i, j: (i, j, 0, 0)),
                 tok(KV_LORA), kr_spec)
    in_specs = [
        tok(d),
        pl.BlockSpec((bb, 1, mod3.shape[2]), lambda i, j: (i, 0, 0)),
        _const_spec((1, d)),
        _const_spec(w["win"].shape),
        _const_spec(w["wa2"].shape), _const_spec((1, HK)),
        _const_spec((1, Q_LORA)), _const_spec((1, KV_LORA)),
        _const_spec(w["wckvT"].shape), _const_spec((KV_LORA, 1)),
        _const_spec(w["wuq"].shape), _const_spec(w["wukT"].shape),
        pl.BlockSpec((tt, 2 * QK_ROPE), lambda i, j: (j, 0)),
    ]
    return pl.pallas_call(
        _inproj_kernel, out_shape=out_shape, grid=grid, in_specs=in_specs, out_specs=out_specs,
        compiler_params=_cparams(("arbitrary", "arbitrary")), name="inproj",
    )(x, mod3, w["g_pre_mix"], w["win"], w["wa2"], w["ba"],
      w["g_mla_q"], w["g_mla_kv"], w["wckvT"], w["g_mla_kvT"], w["wuq"], w["wukT"], cs)


def _gla_kernel(q_ref, k_ref, la_ref, v_ref, r_ref, s0_ref, g_ref,
                o_ref, sfin_ref, s_ref, *, chunk, sub):
    j = pl.program_id(1)
    c_, r_ = chunk, sub
    nsub = c_ // r_
    nseq, tc = q_ref.shape[0], q_ref.shape[1]
    t_in = min(tc, c_)

    @pl.when(j == 0)
    def _():
        s_ref[...] = s0_ref[...]

    row = lax.broadcasted_iota(jnp.int32, (2 * c_, c_), 0)
    col = lax.broadcasted_iota(jnp.int32, (2 * c_, c_), 1)
    bound = jnp.where(row < c_, row + 1, ((row - c_) // r_) * r_)
    lmat = jnp.where(col < bound, 1.0, 0.0).astype(BF16)
    lane = lax.broadcasted_iota(jnp.int32, (1, HK), 1)
    hmask = [(lane >= hh * DK_A) & (lane < (hh + 1) * DK_A) for hh in range(H_A)]
    g_out = g_ref[...]

    def stack_heads(a):
        return jnp.concatenate([jnp.where(hmask[hh], a, 0.0) for hh in range(H_A)], axis=0).astype(BF16)

    def chunk_rows(ref, sq, sl):
        a = ref[sq, sl, :].astype(F32)
        if t_in == c_:
            return a
        return jnp.concatenate([a, jnp.zeros((c_ - t_in, a.shape[1]), F32)], axis=0)

    for c, sq in [(c, sq) for c in range(max(tc // c_, 1)) for sq in range(nseq)]:
        sl = slice(c * c_, c * c_ + t_in)
        g = chunk_rows(la_ref, sq, sl)
        g1 = g.astype(BF16)
        g2 = (g - g1.astype(F32)).astype(BF16)
        bm = _mm(lmat, g1) + _mm(lmat, g2)
        b, mrow = bm[:c_], bm[c_:]
        q = chunk_rows(q_ref, sq, sl)
        k = chunk_rows(k_ref, sq, sl)
        v = chunk_rows(v_ref, sq, sl).astype(BF16)
        s_old = s_ref[sq]

        o_inter = _mm(stack_heads(q * jnp.exp(b)), s_old.astype(BF16))

        qt = q * jnp.exp(b - mrow)
        o_sub = [[] for _ in range(H_A)]
        for i in range(nsub):
            n_k = (i + 1) * r_
            m_i = mrow[i * r_:i * r_ + 1, :]
            kt = (k[:n_k] * jnp.exp(jnp.minimum(m_i - b[:n_k], GLA_EXP_CLAMP))).astype(BF16)
            att = _mm_nt(stack_heads(qt[i * r_:(i + 1) * r_]), kt)
            rr = lax.broadcasted_iota(jnp.int32, (H_A * r_, n_k), 0) % r_
            cc = lax.broadcasted_iota(jnp.int32, (H_A * r_, n_k), 1)
            att = jnp.where(cc <= rr + i * r_, att, 0.0).astype(BF16)
            pv = _mm(att, v[:n_k])
            for hh in range(H_A):
                o_sub[hh].append(pv[hh * r_:(hh + 1) * r_, hh * DV_A:(hh + 1) * DV_A])
        for hh in range(H_A):
            o_h = (o_inter[hh * c_:(hh + 1) * c_] + jnp.concatenate(o_sub[hh], axis=0))[:t_in]
            gate = r_ref[sq, sl, hh * DV_A:(hh + 1) * DV_A].astype(F32)
            o_ref[sq, sl, hh * DV_A:(hh + 1) * DV_A] = (_rms(o_h, g_out) * _silu(gate)).astype(o_ref.dtype)

        bl = b[c_ - 1:c_, :]
        klT = (k * jnp.exp(bl - b)).T.astype(BF16)
        upd = _mm(klT, v)
        decay = jnp.exp(jnp.broadcast_to(bl, (DV_A, HK)).T)
        s_ref[sq] = jnp.concatenate(
            [s_old[hh * DK_A:(hh + 1) * DK_A] * decay[hh * DK_A:(hh + 1) * DK_A]
             + upd[hh * DK_A:(hh + 1) * DK_A, hh * DV_A:(hh + 1) * DV_A] for hh in range(H_A)], axis=0)

    @pl.when(j == pl.num_programs(1) - 1)
    def _():
        sfin_ref[...] = s_ref[...]


def _gla(qa, ka, la, va, ra, s0, g_out, *, chunk, sub, nseq, tc, out_dtype):
    nb, t, _ = qa.shape
    grid = (nb // nseq, t // tc)
    tok = lambda width: pl.BlockSpec((nseq, tc, width), lambda b, j: (b, j, 0))
    st = pl.BlockSpec((nseq, HK, DV_A), lambda b, j: (b, 0, 0))
    return pl.pallas_call(
        functools.partial(_gla_kernel, chunk=chunk, sub=sub),
        out_shape=(jax.ShapeDtypeStruct((nb, t, HV), out_dtype),
                   jax.ShapeDtypeStruct((nb, HK, DV_A), F32)),
        grid=grid,
        in_specs=[tok(HK), tok(HK), tok(HK), tok(HV), tok(HV), st, _const_spec((1, DV_A))],
        out_specs=(tok(HV), st),
        scratch_shapes=[pltpu.VMEM((nseq, HK, DV_A), F32)],
        compiler_params=_cparams(("arbitrary", "arbitrary")), name="gla",
    )(qa, ka, la, va, ra, s0, g_out)


def _flash_kernel(q_ref, k_ref, vT_ref, wuvT_ref, g_ref, o_ref, m_ref, l_ref, acc_ref):
    qi = pl.program_id(1)
    tq = q_ref.shape[2]
    rows = H_B * tq
    q = q_ref[0].reshape(rows, QCAT)
    m_ref[...] = jnp.full(m_ref.shape, -jnp.inf, F32)
    l_ref[...] = jnp.zeros(l_ref.shape, F32)
    acc_ref[...] = jnp.zeros(acc_ref.shape, F32)

    def scores(j):
        kc = k_ref[0, pl.ds(pl.multiple_of(j * tq, tq), tq), :]
        return _mm_nt(kc, q)

    def update(j, st, diagonal):
        if diagonal:
            t_k = lax.broadcasted_iota(jnp.int32, (tq, rows), 0)
            t_q = lax.broadcasted_iota(jnp.int32, (tq, rows), 1) % tq
            st = jnp.where(t_k <= t_q, st, -jnp.inf)
        m_prev = m_ref[...]
        m_new = jnp.maximum(m_prev, jnp.max(st, axis=0, keepdims=True))
        alpha = jnp.exp2(m_prev - m_new)
        p = jnp.exp2(st - m_new)
        l_ref[...] = alpha * l_ref[...] + jnp.sum(p, axis=0, keepdims=True)
        acc_ref[...] = alpha * acc_ref[...] + _mm(vT_ref[0, j], p.astype(BF16))
        m_ref[...] = m_new

    def full_block(j, st):
        st_next = scores(j + 1)
        update(j, st, False)
        return st_next

    update(qi, lax.fori_loop(0, qi, full_block, scores(0)), True)

    o_lat_t = (acc_ref[...] / l_ref[...]).astype(BF16)
    for hh in range(H_B):
        ob_t = _mm(wuvT_ref[hh], o_lat_t[:, hh * tq:(hh + 1) * tq])
        y_t = ob_t * lax.rsqrt(jnp.mean(ob_t * ob_t, axis=0, keepdims=True) + EPS) * g_ref[...]
        o_ref[0, :, hh * V_B:(hh + 1) * V_B] = y_t.T.astype(o_ref.dtype)


def _flash(qcat, kcat, lat_t, wuv_t, g_col, *, tq):
    nb, _, t, _ = qcat.shape
    rows = H_B * tq
    return pl.pallas_call(
        _flash_kernel,
        out_shape=jax.ShapeDtypeStruct((nb, t, H_B * V_B), BF16),
        grid=(nb, t // tq),
        in_specs=[pl.BlockSpec((1, H_B, tq, QCAT), lambda b, i: (b, 0, i, 0)),
                  pl.BlockSpec((1, t, QCAT), lambda b, i: (b, 0, 0)),
                  pl.BlockSpec((1, t // tq, KV_LORA, tq), lambda b, i: (b, 0, 0, 0)),
                  _const_spec((H_B, V_B, KV_LORA)), _const_spec((V_B, 1))],
        out_specs=pl.BlockSpec((1, tq, H_B * V_B), lambda b, i: (b, i, 0)),
        scratch_shapes=[pltpu.VMEM((1, rows), F32), pltpu.VMEM((1, rows), F32),
                        pltpu.VMEM((KV_LORA, rows), F32)],
        compiler_params=_cparams(("arbitrary", "arbitrary")), name="flash",
    )(qcat, kcat, lat_t, wuv_t, g_col)


def _decode_kernel(pt_ref, q_ref, knew_ref, wuv_ref, g_ref, lat_hbm, krT_hbm, o_ref,
                   lat_buf, kr_buf, s_scr, p_scr, sems, *, n_pages, td, ck):
    b = pl.program_id(0)
    slot = b % 2
    page = lat_hbm.shape[1]
    nck = (n_pages * page) // ck
    ppc = ck // page

    def page_copies(seq, sl, p):
        pg = pt_ref[seq * n_pages + p]
        rows = pl.ds(pl.multiple_of(p * page, page), page)
        return (pltpu.make_async_copy(lat_hbm.at[pg], lat_buf.at[sl, rows, :], sems.at[0, sl]),
                pltpu.make_async_copy(krT_hbm.at[pg], kr_buf.at[sl, p], sems.at[1, sl]))

    def fetch(seq, sl):
        def body(p, carry):
            for cp in page_copies(seq, sl, p):
                cp.start()
            return carry
        lax.fori_loop(0, n_pages, body, 0, unroll=8)

    @pl.when(b == 0)
    def _():
        fetch(0, 0)

    @pl.when(b + 1 < pl.num_programs(0))
    def _():
        fetch(b + 1, 1 - slot)

    pltpu.make_async_copy(lat_buf.at[slot], lat_buf.at[slot], sems.at[0, slot]).wait()
    pltpu.make_async_copy(kr_buf.at[slot], kr_buf.at[slot], sems.at[1, slot]).wait()

    q = q_ref[0].astype(BF16)
    ql, qr = q[:, 0:KV_LORA], q[:, KV_LORA:KV_LORA + QK_ROPE]

    def lat_chunk(c):
        return lat_buf[slot, c * ck:(c + 1) * ck, :].astype(BF16)

    for c in range(nck):
        s_rope = jnp.concatenate(
            [_mm(qr, kr_buf[slot, c * ppc + i].astype(BF16)) for i in range(ppc)], axis=1)
        s_scr[c] = _mm_nt(ql, lat_chunk(c)) + s_rope

    kn = jnp.concatenate([knew_ref[0], jnp.zeros((LANE - td, QCAT), F32)], axis=0).astype(BF16)
    s_new = _mm_nt(q, kn)
    t_q = lax.broadcasted_iota(jnp.int32, s_new.shape, 0) % td
    t_k = lax.broadcasted_iota(jnp.int32, s_new.shape, 1)
    s_new = jnp.where(t_k <= t_q, s_new, -jnp.inf)
    s_all = s_scr[...]
    m = jnp.maximum(jnp.max(jnp.max(s_all, axis=0), axis=-1, keepdims=True),
                    jnp.max(s_new, axis=-1, keepdims=True))
    p_all = jnp.exp2(s_all - m[None])
    p_new = jnp.exp2(s_new - m)
    l = jnp.sum(jnp.sum(p_all, axis=0), axis=-1, keepdims=True) + jnp.sum(p_new, axis=-1, keepdims=True)
    p_scr[...] = p_all.astype(BF16)

    parts = [_mm(p_new.astype(BF16), kn[:, 0:KV_LORA])] + [_mm(p_scr[c], lat_chunk(c)) for c in range(nck)]
    while len(parts) > 1:
        parts = [parts[i] + parts[i + 1] if i + 1 < len(parts) else parts[i] for i in range(0, len(parts), 2)]
    o_lat = (parts[0] / l).astype(BF16)
    ob = _mm(o_lat, wuv_ref[...])
    for hh in range(H_B):
        blk = ob[hh * td:(hh + 1) * td, hh * V_B:(hh + 1) * V_B]
        o_ref[0, :, hh * V_B:(hh + 1) * V_B] = _rms(blk, g_ref[...]).astype(o_ref.dtype)


def _decode(qcat, knew, wuv_all, g_out, cache_lat, cache_krT, page_table, *, ck):
    nb, _, td, _ = qcat.shape
    qcat = qcat.reshape(nb, H_B * td, QCAT)
    n_pages = page_table.shape[1]
    page = cache_lat.shape[1]
    past = n_pages * page
    rows = H_B * td
    grid_spec = pltpu.PrefetchScalarGridSpec(
        num_scalar_prefetch=1,
        grid=(nb,),
        in_specs=[pl.BlockSpec((1, rows, QCAT), lambda b, pt: (b, 0, 0)),
                  pl.BlockSpec((1, td, QCAT), lambda b, pt: (b, 0, 0)),
                  pl.BlockSpec(wuv_all.shape, lambda b, pt: (0, 0)),
                  pl.BlockSpec((1, V_B), lambda b, pt: (0, 0)),
                  pl.BlockSpec(memory_space=pl.ANY),
                  pl.BlockSpec(memory_space=pl.ANY)],
        out_specs=pl.BlockSpec((1, td, H_B * V_B), lambda b, pt: (b, 0, 0)),
        scratch_shapes=[pltpu.VMEM((2, past, KV_LORA), F32),
                        pltpu.VMEM((2, n_pages, QK_ROPE, page), F32),
                        pltpu.VMEM((past // ck, rows, ck), F32),
                        pltpu.VMEM((past // ck, rows, ck), BF16),
                        pltpu.SemaphoreType.DMA((2, 2))],
    )
    return pl.pallas_call(
        functools.partial(_decode_kernel, n_pages=n_pages, td=td, ck=ck),
        out_shape=jax.ShapeDtypeStruct((nb, td, H_B * V_B), F32),
        grid_spec=grid_spec,
        compiler_params=_cparams(("arbitrary",)), name="decode",
    )(page_table.reshape(-1), qcat, knew, wuv_all, g_out, cache_lat, cache_krT)


def _outproj_kernel(oa_ref, ob_ref, x_ref, mod_ref, wo_ref, gpost_ref, gpre_ref, x1_ref, h2_ref):
    bb, tt, d = x_ref.shape
    m = bb * tt
    mod = mod_ref[...]
    gt1, sh2, sc2 = mod[:, :, 2 * d:3 * d], mod[:, :, 3 * d:4 * d], mod[:, :, 4 * d:5 * d]
    mix = (_mm(oa_ref[...].reshape(m, HV).astype(BF16), wo_ref[0:HV, :])
           + _mm(ob_ref[...].reshape(m, H_B * V_B).astype(BF16), wo_ref[HV:HV + H_B * V_B, :]))
    x1 = x_ref[...] + gt1 * _rms(mix, gpost_ref[...]).reshape(bb, tt, d)
    x1_ref[...] = x1
    h2_ref[...] = (_rms(x1, gpre_ref[...]) * (1.0 + sc2) + sh2).astype(h2_ref.dtype)


def _outproj(oa, ob, x, mod3, w, *, bb, tt, act_dtype):
    nb, t, d = x.shape
    grid = (nb // bb, t // tt)
    tok = lambda width: pl.BlockSpec((bb, tt, width), lambda i, j: (i, j, 0))
    return pl.pallas_call(
        _outproj_kernel,
        out_shape=(jax.ShapeDtypeStruct((nb, t, d), F32), jax.ShapeDtypeStruct((nb, t, d), act_dtype)),
        grid=grid,
        in_specs=[tok(HV), tok(H_B * V_B), tok(d),
                  pl.BlockSpec((bb, 1, mod3.shape[2]), lambda i, j: (i, 0, 0)),
                  _const_spec(w["wo"].shape), _const_spec((1, d)), _const_spec((1, d))],
        out_specs=(tok(d), tok(d)),
        compiler_params=_cparams(("arbitrary", "arbitrary")), name="outproj",
    )(oa, ob, x, mod3, w["wo"], w["g_post_mix"], w["g_pre_ffn"])


def _gelu_tanh(x):
    c = math.sqrt(2.0 / math.pi)
    hx = 0.5 * x
    return hx + hx * jnp.tanh(x * (c + (c * 0.044715) * (x * x)))


def _ffn_kernel(h2_ref, x1_ref, mod_ref, cst_ref, wup_ref, wc_ref, bc_ref, wdn_ref, gpost_ref,
                y_ref, cnew_ref, carry_ref):
    bb, tt, d = x1_ref.shape
    m = bb * tt
    fc = FFN_CHUNK
    nc = wdn_ref.shape[0] // fc

    @pl.when(pl.program_id(1) == 0)
    def _():
        carry_ref[...] = cst_ref[...]

    assert bb == 1 or tt == SUBLANE
    ng = m // SUBLANE
    h2 = h2_ref[...].reshape(m, d).astype(BF16)
    sub = lax.broadcasted_iota(jnp.int32, (1, SUBLANE, fc), 1)

    def up_conv(k):
        cols = slice(k * fc, (k + 1) * fc)
        u = _mm(h2, wup_ref[:, cols])
        prev = carry_ref[:, :, cols]
        tail = u.reshape(bb, tt, fc)[:, tt - (CONV_W - 1):tt]
        carry_ref[:, :, cols] = tail
        cnew_ref[:, :, cols] = tail
        g3 = u.reshape(ng, SUBLANE, fc)
        r1, r2 = pltpu.roll(g3, 1, axis=1), pltpu.roll(g3, 2, axis=1)
        p1 = jnp.broadcast_to(prev[:, 1:2], (bb, SUBLANE, fc))
        p2 = jnp.where(sub == 0, jnp.broadcast_to(prev[:, 0:1], (bb, SUBLANE, fc)), p1)
        if bb == 1:
            p1 = jnp.concatenate([p1, r1[:ng - 1]], axis=0)
            p2 = jnp.concatenate([p2, r2[:ng - 1]], axis=0)
        u1 = jnp.where(sub < 1, p1, r1).reshape(m, fc)
        u2 = jnp.where(sub < 2, p2, r2).reshape(m, fc)
        wc = wc_ref[:, cols]
        return bc_ref[:, cols] + wc[2:3] * u + wc[1:2] * u1 + wc[0:1] * u2

    acc = jnp.zeros((m, d), F32)
    ready = [(up_conv(c), up_conv(nc + c)) for c in range(min(FFN_AHEAD, nc))]
    for c in range(nc):
        if c + FFN_AHEAD < nc:
            ready.append((up_conv(c + FFN_AHEAD), up_conv(nc + c + FFN_AHEAD)))
        val, gate = ready.pop(0)
        act = (val * _gelu_tanh(gate)).astype(BF16)
        acc = acc + _mm(act, wdn_ref[c * fc:(c + 1) * fc, :])
    gt2 = mod_ref[...][:, :, 5 * d:6 * d]
    y_ref[...] = x1_ref[...] + gt2 * _rms(acc, gpost_ref[...]).reshape(bb, tt, d)


def _ffn(h2, x1, mod3, cstate, w, *, bb, tt):
    nb, t, d = x1.shape
    f2 = cstate.shape[2]
    grid = (nb // bb, t // tt)
    tok = lambda width: pl.BlockSpec((bb, tt, width), lambda i, j: (i, j, 0))
    cs_spec = pl.BlockSpec((bb, CONV_W - 1, f2), lambda i, j: (i, 0, 0))
    return pl.pallas_call(
        _ffn_kernel,
        out_shape=(jax.ShapeDtypeStruct((nb, t, d), F32),
                   jax.ShapeDtypeStruct((nb, CONV_W - 1, f2), F32)),
        grid=grid,
        in_specs=[tok(d), tok(d),
                  pl.BlockSpec((bb, 1, mod3.shape[2]), lambda i, j: (i, 0, 0)),
                  cs_spec,
                  _const_spec(w["wup"].shape), _const_spec(w["wconv"].shape), _const_spec(w["bconv"].shape),
                  _const_spec(w["wdn"].shape), _const_spec((1, d))],
        out_specs=(tok(d), cs_spec),
        scratch_shapes=[pltpu.VMEM((bb, CONV_W - 1, f2), F32)],
        compiler_params=_cparams(("arbitrary", "arbitrary")), name="ffn",
    )(h2, x1, mod3, cstate, w["wup"], w["wconv"], w["bconv"], w["wdn"], w["g_post_ffn"])


def _rot_cols(wr):
    half = wr.shape[-1] // 2
    return jnp.concatenate([-wr[..., half:], wr[..., :half]], axis=-1)


def _prep_weights(l, w_in, w_gla_a2, b_gla_a, g_gla_out, g_mla_q, g_mla_kv, w_mla_uq, w_mla_uk, w_mla_uv,
                  g_mla_out, w_o, w_ffn_up, w_ffn_conv, b_ffn_conv, w_ffn_down, g_pre_mix, g_post_mix,
                  g_pre_ffn, g_post_ffn):
    wi = w_in[l]
    d = wi.shape[0]
    o_gr = 2 * HK + 2 * HV
    o_cq = o_gr + GATE_RANK
    o_ckv = o_cq + Q_LORA
    o_kr = o_ckv + KV_LORA
    kr = wi[:, o_kr:o_kr + QK_ROPE]
    win = jnp.concatenate(
        [wi[:, 0:o_gr], wi[:, o_cq:o_ckv], wi[:, o_ckv:o_kr], kr, _rot_cols(kr),
         wi[:, o_gr:o_cq], jnp.zeros((d, LANE - GATE_RANK), wi.dtype)], axis=1).astype(BF16)
    wa2 = jnp.concatenate([w_gla_a2[l], jnp.zeros((LANE - GATE_RANK, HK), F32)], axis=0).astype(BF16)
    uq = w_mla_uq[l].reshape(Q_LORA, H_B, QK_NOPE + QK_ROPE)
    uq_rope = uq[:, :, QK_NOPE:]
    wuq = jnp.concatenate(
        [uq[:, :, :QK_NOPE].reshape(Q_LORA, H_B * QK_NOPE),
         jnp.concatenate([uq_rope, _rot_cols(uq_rope)], axis=-1).reshape(Q_LORA, H_B * 2 * QK_ROPE)],
        axis=1).astype(BF16)
    return dict(
        win=win, wa2=wa2, ba=b_gla_a[l].reshape(1, HK),
        g_gla_out=g_gla_out[l].reshape(1, DV_A),
        g_mla_q=g_mla_q[l].reshape(1, Q_LORA), g_mla_kv=g_mla_kv[l].reshape(1, KV_LORA),
        wuq=wuq, wukT=jnp.transpose(w_mla_uk[l], (1, 2, 0)).astype(BF16),
        wckvT=wi[:, o_ckv:o_kr].T.astype(BF16), g_mla_kvT=g_mla_kv[l].reshape(KV_LORA, 1),
        wuvT=jnp.transpose(w_mla_uv[l], (1, 2, 0)).astype(BF16),
        g_mla_outT=g_mla_out[l].reshape(V_B, 1),
        wuv_all=w_mla_uv[l].reshape(KV_LORA, H_B * V_B).astype(BF16),
        g_mla_out=g_mla_out[l].reshape(1, V_B),
        wo=w_o[l].astype(BF16),
        wup=w_ffn_up[l].astype(BF16), wconv=w_ffn_conv[l], bconv=b_ffn_conv[l].reshape(1, -1),
        wdn=w_ffn_down[l].astype(BF16),
        g_pre_mix=g_pre_mix[l].reshape(1, d), g_post_mix=g_post_mix[l].reshape(1, d),
        g_pre_ffn=g_pre_ffn[l].reshape(1, d), g_post_ffn=g_post_ffn[l].reshape(1, d),
    )


def _rope_table(pos):
    inv = ROPE_BASE ** (-jnp.arange(0, QK_ROPE, 2, dtype=F32) / QK_ROPE)
    ang = pos.astype(F32)[:, None] * inv[None, :]
    cos, sin = jnp.cos(ang), jnp.sin(ang)
    return jnp.concatenate([cos, cos, sin, sin], axis=-1)


def _token_tile(t, target):
    tt = min(t, target)
    while t % tt:
        tt //= 2
    return tt


def _layer(x, mod3, cs, w, gla_s0, conv_state, *, per_seq, attend):
    nb, t, d = x.shape
    if per_seq:
        bb, tt = nb, t
    else:
        bb, tt = 1, _token_tile(t, 512)
    act_dtype = F32 if per_seq else BF16
    tkb = _token_tile(bb * tt, ATTN_TILE)
    qa, ka, va, ra, la, qcat, kcat, lat_t, lat, kr = _inproj(
        x, mod3, cs, w, bb=bb, tt=tt, tkb=tkb, act_dtype=act_dtype)

    if per_seq:
        o_a, s_fin = _gla(qa, ka, la, va, ra, gla_s0, w["g_gla_out"], chunk=GLA_CHUNK_SHORT,
                          sub=GLA_CHUNK_SHORT, nseq=_token_tile(nb, 8), tc=t, out_dtype=act_dtype)
    else:
        o_a, s_fin = _gla(qa, ka, la, va, ra, gla_s0, w["g_gla_out"], chunk=LANE, sub=LANE // 4,
                          nseq=1, tc=_token_tile(t, 512), out_dtype=act_dtype)

    if not per_seq:
        kr = jnp.swapaxes(kr, 1, 2)
    o_b = attend(qcat, kcat, lat_t)
    x1, h2 = _outproj(o_a, o_b, x, mod3, w, bb=bb, tt=tt, act_dtype=act_dtype)
    ffn_tt = tt if per_seq else _token_tile(t, FFN_ROWS)
    y, c_new = _ffn(h2, x1, mod3, conv_state, w, bb=bb, tt=ffn_tt)
    return y, lat, kr, s_fin, c_new


def kernel(x_prompt, x_sample, c_prompt, c_sample, cache_latent, cache_krope, state_gla, state_conv, page_table, w_ada, b_ada, g_pre_mix, g_post_mix, g_pre_ffn, g_post_ffn, w_in, w_gla_a2, b_gla_a, g_gla_out, g_mla_q, g_mla_kv, w_mla_uq, w_mla_uk, w_mla_uv, g_mla_out, w_o, w_ffn_up, w_ffn_conv, b_ffn_conv, w_ffn_down):
    depth = w_ada.shape[0]
    nbp, tp, d = x_prompt.shape
    nbs, ts, _ = x_sample.shape
    n_pages, page = page_table.shape[1], cache_latent.shape[2]
    past_len = n_pages * page
    f2 = w_ffn_up.shape[2]
    cs_p = _rope_table(jnp.arange(tp, dtype=jnp.int32))
    cs_s = _rope_table(past_len + jnp.arange(ts, dtype=jnp.int32))

    hp, hs = x_prompt, x_sample
    outs = [[] for _ in range(8)]
    for l in range(depth):
        w = _prep_weights(l, w_in, w_gla_a2, b_gla_a, g_gla_out, g_mla_q, g_mla_kv, w_mla_uq, w_mla_uk,
                          w_mla_uv, g_mla_out, w_o, w_ffn_up, w_ffn_conv, b_ffn_conv, w_ffn_down,
                          g_pre_mix, g_post_mix, g_pre_ffn, g_post_ffn)
        mod = _ada(jnp.concatenate([c_prompt, c_sample], axis=0), w_ada[l], b_ada[l])
        mod3 = mod.reshape(nbp + nbs, 1, mod.shape[1])

        def attend_prompt(qcat, kcat, lat_t):
            return _flash(qcat, kcat, lat_t, w["wuvT"], w["g_mla_outT"], tq=_token_tile(tp, ATTN_TILE))

        def attend_sample(qcat, kcat, lat_t):
            return _decode(qcat, kcat, w["wuv_all"], w["g_mla_out"], cache_latent[l],
                           jnp.swapaxes(cache_krope[l], 1, 2), page_table, ck=8 * page)

        hp, a1, a2, a3, a4 = _layer(
            hp, mod3[:nbp], cs_p, w, jnp.zeros((nbp, HK, DV_A), F32),
            jnp.zeros((nbp, CONV_W - 1, f2), F32), per_seq=False, attend=attend_prompt)
        hs, b1, b2, b3, b4 = _layer(
            hs, mod3[nbp:], cs_s, w, state_gla[l].reshape(nbs, HK, DV_A), state_conv[l],
            per_seq=True, attend=attend_sample)
        for lst, val in zip(outs, (a1, a2, a3.reshape(nbp, H_A, DK_A, DV_A), a4,
                                   b1, b2, b3.reshape(nbs, H_A, DK_A, DV_A), b4)):
            lst.append(val)
    return (hp, hs) + tuple(jnp.stack(o) for o in outs)
```

```python
import functools
import math

import jax
import jax.numpy as jnp
import numpy as np
from jax import lax
from jax.experimental import pallas as pl
from jax.experimental.pallas import tpu as pltpu

F32 = jnp.float32
BF16 = jnp.bfloat16

EPS = 1e-6
H_A, DK_A, DV_A = 4, 64, 128
GATE_RANK = 16
GATE_TAU = 16.0
H_B = 4
Q_LORA, KV_LORA = 384, 256
QK_NOPE, QK_ROPE, V_B = 128, 64, 128
ROPE_BASE = 10000.0
CONV_W = 3

HK = H_A * DK_A
HV = H_A * DV_A
QCAT = KV_LORA + 2 * QK_ROPE
QX = QK_NOPE + 2 * QK_ROPE
LANE = 128
SUBLANE = 8

_O_QA, _O_KA, _O_VA, _O_RA = 0, HK, 2 * HK, 2 * HK + HV
_O_CQ = 2 * HK + 2 * HV
_O_CKV = _O_CQ + Q_LORA
_O_KRR = _O_CKV + KV_LORA
_O_GR = _O_KRR + 2 * QK_ROPE
IN_EXT = _O_GR + LANE

QK_SCALE_LOG2E = (QK_NOPE + QK_ROPE) ** -0.5 * math.log2(math.e)

GLA_CHUNK_SHORT = 16
ATTN_TILE = 256
DECODE_SEGMENTS = 4
FFN_CHUNK = 256
FFN_ROWS = 256
FFN_AHEAD = 4
VMEM_LIMIT = 56 * 1024 * 1024
GLA_EXP_CLAMP = 80.0


def _mm(a, b):
    return jnp.dot(a, b, preferred_element_type=F32)


def _mm_nt(a, b):
    return lax.dot_general(a, b, (((1,), (1,)), ((), ())), preferred_element_type=F32)


def _rms(x, g):
    return x * lax.rsqrt(jnp.mean(x * x, axis=-1, keepdims=True) + EPS) * g


def _silu(x):
    return x / (1.0 + jnp.exp(-x))


def _log_sigmoid(x):
    return -(jnp.maximum(-x, 0.0) + jnp.log(1.0 + jnp.exp(-jnp.abs(x))))


def _cparams(sem):
    return pltpu.CompilerParams(dimension_semantics=sem, vmem_limit_bytes=VMEM_LIMIT)


def _const_spec(shape):
    nd = len(shape)
    return pl.BlockSpec(shape, lambda *_: (0,) * nd)


def _ada_kernel(c_ref, w_ref, b_ref, o_ref):
    s = _silu(c_ref[...]).astype(BF16)
    o_ref[...] = _mm(s, w_ref[...].astype(BF16)) + b_ref[...]


def _ada(c_all, w_ada, b_ada):
    n, d = c_all.shape
    nout = w_ada.shape[1]
    tn = d
    return pl.pallas_call(
        _ada_kernel,
        out_shape=jax.ShapeDtypeStruct((n, nout), F32),
        grid=(nout // tn,),
        in_specs=[pl.BlockSpec((n, d), lambda j: (0, 0)),
                  pl.BlockSpec((d, tn), lambda j: (0, j)),
                  pl.BlockSpec((1, tn), lambda j: (0, j))],
        out_specs=pl.BlockSpec((n, tn), lambda j: (0, j)),
        compiler_params=_cparams(("arbitrary",)),
        name="ada",
    )(c_all, w_ada, b_ada.reshape(1, nout))


def _rope_sum(a3, cs):
    bb, tt, w = a3.shape
    a = (a3 * cs[None]).reshape(bb * tt, w)
    return a + pltpu.roll(a, QK_ROPE, axis=1)


def _inproj_kernel(x_ref, mod_ref, gpre_ref, win_ref, wa2_ref, ba_ref, gq_ref, gkv_ref, wuq_ref, cs_ref,
                   *rest, expanded):
    if expanded:
        wukx_ref, wuvT_ref = rest[:2]
        qa_ref, ka_ref, va_ref, ra_ref, la_ref, qx_ref, kx_ref, vT_ref, lat_ref, kr_ref = rest[2:]
    else:
        (wukT_ref,) = rest[:1]
        qa_ref, ka_ref, va_ref, ra_ref, la_ref, qcat_ref, kcat_ref, lat_ref, kr_ref = rest[1:]
    bb, tt, d = x_ref.shape
    m = bb * tt
    x = x_ref[...]
    mod = mod_ref[...]
    sh1, sc1 = mod[:, :, 0:d], mod[:, :, d:2 * d]
    h = _rms(x, gpre_ref[...]) * (1.0 + sc1) + sh1
    h = h.reshape(m, d).astype(BF16)
    z = _mm(h, win_ref[...])

    qa_ref[...] = (z[:, _O_QA:_O_QA + HK] * (DK_A ** -0.5)).reshape(bb, tt, HK)
    ka_ref[...] = z[:, _O_KA:_O_KA + HK].reshape(bb, tt, HK)
    va_ref[...] = z[:, _O_VA:_O_VA + HV].astype(va_ref.dtype).reshape(bb, tt, HV)
    ra_ref[...] = z[:, _O_RA:_O_RA + HV].astype(ra_ref.dtype).reshape(bb, tt, HV)
    gr = z[:, _O_GR:_O_GR + LANE].astype(BF16)
    la = _log_sigmoid(_mm(gr, wa2_ref[...]) + ba_ref[...]) * (1.0 / GATE_TAU)
    la_ref[...] = la.reshape(bb, tt, HK)

    cs = cs_ref[...]
    ckv = _rms(z[:, _O_CKV:_O_CKV + KV_LORA], gkv_ref[...])
    lat_ref[...] = ckv.reshape(bb, tt, KV_LORA)
    krs = _rope_sum(z[:, _O_KRR:_O_KRR + 2 * QK_ROPE].reshape(bb, tt, 2 * QK_ROPE), cs)
    if len(kr_ref.shape) == 2:
        kr_ref[...] = krs.T[0:QK_ROPE, :]
    else:
        kr_ref[...] = krs[:, 0:QK_ROPE].reshape(bb, tt, QK_ROPE)
    lane = lax.broadcasted_iota(jnp.int32, (m, 2 * QK_ROPE), 1)
    kr0 = jnp.where(lane < QK_ROPE, krs, 0.0)
    cq = _rms(z[:, _O_CQ:_O_CQ + Q_LORA], gq_ref[...]).astype(BF16)
    qb = _mm(cq, wuq_ref[...])

    def q_rope(hh):
        o = H_B * QK_NOPE + hh * 2 * QK_ROPE
        return _rope_sum(qb[:, o:o + 2 * QK_ROPE].reshape(bb, tt, 2 * QK_ROPE), cs) * QK_SCALE_LOG2E

    if expanded:
        ckv_b = ckv.astype(BF16)
        kn = _mm(ckv_b, wukx_ref[...])
        vt = _mm_nt(wuvT_ref[...], ckv_b)
        tkb = vT_ref.shape[2]
        for c in range(vT_ref.shape[0]):
            vT_ref[c] = vt[:, c * tkb:(c + 1) * tkb].astype(vT_ref.dtype)
        for hh in range(H_B):
            nope = slice(hh * QK_NOPE, (hh + 1) * QK_NOPE)
            kx_ref[:, hh, :, 0:QK_NOPE] = kn[:, nope].astype(kx_ref.dtype).reshape(bb, tt, QK_NOPE)
            kx_ref[:, hh, :, QK_NOPE:QX] = kr0.astype(kx_ref.dtype).reshape(bb, tt, 2 * QK_ROPE)
            qx_ref[:, hh, :, 0:QK_NOPE] = (qb[:, nope] * QK_SCALE_LOG2E).astype(qx_ref.dtype).reshape(bb, tt, QK_NOPE)
            qx_ref[:, hh, :, QK_NOPE:QX] = q_rope(hh).astype(qx_ref.dtype).reshape(bb, tt, 2 * QK_ROPE)
    else:
        kcat_ref[:, :, 0:KV_LORA] = ckv.astype(kcat_ref.dtype).reshape(bb, tt, KV_LORA)
        kcat_ref[:, :, KV_LORA:QCAT] = kr0.astype(kcat_ref.dtype).reshape(bb, tt, 2 * QK_ROPE)
        for hh in range(H_B):
            qn = qb[:, hh * QK_NOPE:(hh + 1) * QK_NOPE].astype(BF16)
            qlat = _mm(qn, wukT_ref[hh]) * QK_SCALE_LOG2E
            qcat_ref[:, hh, :, 0:KV_LORA] = qlat.astype(qcat_ref.dtype).reshape(bb, tt, KV_LORA)
            qcat_ref[:, hh, :, KV_LORA:QCAT] = q_rope(hh).astype(qcat_ref.dtype).reshape(bb, tt, 2 * QK_ROPE)


def _inproj(x, mod3, cs, w, *, bb, tt, tkb, act_dtype, expanded):
    nb, t, d = x.shape
    m = bb * tt
    grid = (nb // bb, t // tt)
    tok = lambda width: pl.BlockSpec((bb, tt, width), lambda i, j: (i, j, 0))
    heads = lambda width: pl.BlockSpec((bb, H_B, tt, width), lambda i, j: (i, 0, j, 0))
    if expanded:
        mla_w = (w["wukx"], w["wuvT"])
        mla_shape = (jax.ShapeDtypeStruct((nb, H_B, t, QX), act_dtype),
                     jax.ShapeDtypeStruct((nb, H_B, t, QX), act_dtype),
                     jax.ShapeDtypeStruct((nb // bb, (t // tt) * (m // tkb), H_B * V_B, tkb), BF16))
        mla_specs = (heads(QX), heads(QX),
                     pl.BlockSpec((None, m // tkb, H_B * V_B, tkb), lambda i, j: (i, j, 0, 0)))
    else:
        mla_w = (w["wukT"],)
        mla_shape = (jax.ShapeDtypeStruct((nb, H_B, t, QCAT), act_dtype),
                     jax.ShapeDtypeStruct((nb, t, QCAT), act_dtype))
        mla_specs = (heads(QCAT), tok(QCAT))
    kr_shape = jax.ShapeDtypeStruct((nb, QK_ROPE, t) if bb == 1 else (nb, t, QK_ROPE), F32)
    kr_spec = pl.BlockSpec((None, QK_ROPE, tt), lambda i, j: (i, 0, j)) if bb == 1 else tok(QK_ROPE)
    out_shape = (
        jax.ShapeDtypeStruct((nb, t, HK), F32),
        jax.ShapeDtypeStruct((nb, t, HK), F32),
        jax.ShapeDtypeStruct((nb, t, HV), act_dtype),
        jax.ShapeDtypeStruct((nb, t, HV), act_dtype),
        jax.ShapeDtypeStruct((nb, t, HK), F32),
    ) + mla_shape + (jax.ShapeDtypeStruct((nb, t, KV_LORA), F32), kr_shape)
    out_specs = (tok(HK), tok(HK), tok(HV), tok(HV), tok(HK)) + mla_specs + (tok(KV_LORA), kr_spec)
    in_specs = [
        tok(d),
        pl.BlockSpec((bb, 1, mod3.shape[2]), lambda i, j: (i, 0, 0)),
        _const_spec((1, d)),
        _const_spec(w["win"].shape),
        _const_spec(w["wa2"].shape), _const_spec((1, HK)),
        _const_spec((1, Q_LORA)), _const_spec((1, KV_LORA)),
        _const_spec(w["wuq"].shape),
        pl.BlockSpec((tt, 2 * QK_ROPE), lambda i, j: (j, 0)),
    ] + [_const_spec(a.shape) for a in mla_w]
    return pl.pallas_call(
        functools.partial(_inproj_kernel, expanded=expanded),
        out_shape=out_shape, grid=grid, in_specs=in_specs, out_specs=out_specs,
        compiler_params=_cparams(("arbitrary", "arbitrary")), name="inproj",
    )(x, mod3, w["g_pre_mix"], w["win"], w["wa2"], w["ba"], w["g_mla_q"], w["g_mla_kv"], w["wuq"], cs, *mla_w)


def _gla_kernel(q_ref, k_ref, la_ref, v_ref, r_ref, s0_ref, g_ref,
                o_ref, sfin_ref, s_ref, *, chunk, sub):
    j = pl.program_id(1)
    c_, r_ = chunk, sub
    nsub = c_ // r_
    nseq, tc = q_ref.shape[0], q_ref.shape[1]
    t_in = min(tc, c_)

    @pl.when(j == 0)
    def _():
        s_ref[...] = s0_ref[...]

    row = lax.broadcasted_iota(jnp.int32, (2 * c_, c_), 0)
    col = lax.broadcasted_iota(jnp.int32, (2 * c_, c_), 1)
    bound = jnp.where(row < c_, row + 1, ((row - c_) // r_) * r_)
    lmat = jnp.where(col < bound, 1.0, 0.0).astype(BF16)
    lane = lax.broadcasted_iota(jnp.int32, (1, HK), 1)
    hmask = [(lane >= hh * DK_A) & (lane < (hh + 1) * DK_A) for hh in range(H_A)]
    g_out = g_ref[...]

    def stack_heads(a):
        return jnp.concatenate([jnp.where(hmask[hh], a, 0.0) for hh in range(H_A)], axis=0).astype(BF16)

    def chunk_rows(ref, sq, sl):
        a = ref[sq, sl, :].astype(F32)
        if t_in == c_:
            return a
        return jnp.concatenate([a, jnp.zeros((c_ - t_in, a.shape[1]), F32)], axis=0)

    for c, sq in [(c, sq) for c in range(max(tc // c_, 1)) for sq in range(nseq)]:
        sl = slice(c * c_, c * c_ + t_in)
        g = chunk_rows(la_ref, sq, sl)
        g1 = g.astype(BF16)
        g2 = (g - g1.astype(F32)).astype(BF16)
        bm = _mm(lmat, g1) + _mm(lmat, g2)
        b, mrow = bm[:c_], bm[c_:]
        q = chunk_rows(q_ref, sq, sl)
        k = chunk_rows(k_ref, sq, sl)
        v = chunk_rows(v_ref, sq, sl).astype(BF16)
        s_old = s_ref[sq]

        o_inter = _mm(stack_heads(q * jnp.exp(b)), s_old.astype(BF16))

        qt = q * jnp.exp(b - mrow)
        o_sub = [[] for _ in range(H_A)]
        for i in range(nsub):
            n_k = (i + 1) * r_
            m_i = mrow[i * r_:i * r_ + 1, :]
            kt = (k[:n_k] * jnp.exp(jnp.minimum(m_i - b[:n_k], GLA_EXP_CLAMP))).astype(BF16)
            att = _mm_nt(stack_heads(qt[i * r_:(i + 1) * r_]), kt)
            rr = lax.broadcasted_iota(jnp.int32, (H_A * r_, n_k), 0) % r_
            cc = lax.broadcasted_iota(jnp.int32, (H_A * r_, n_k), 1)
            att = jnp.where(cc <= rr + i * r_, att, 0.0).astype(BF16)
            pv = _mm(att, v[:n_k])
            for hh in range(H_A):
                o_sub[hh].append(pv[hh * r_:(hh + 1) * r_, hh * DV_A:(hh + 1) * DV_A])
        for hh in range(H_A):
            o_h = (o_inter[hh * c_:(hh + 1) * c_] + jnp.concatenate(o_sub[hh], axis=0))[:t_in]
            gate = r_ref[sq, sl, hh * DV_A:(hh + 1) * DV_A].astype(F32)
            o_ref[sq, sl, hh * DV_A:(hh + 1) * DV_A] = (_rms(o_h, g_out) * _silu(gate)).astype(o_ref.dtype)

        bl = b[c_ - 1:c_, :]
        klT = (k * jnp.exp(bl - b)).T.astype(BF16)
        upd = _mm(klT, v)
        decay = jnp.exp(jnp.broadcast_to(bl, (DV_A, HK)).T)
        s_ref[sq] = jnp.concatenate(
            [s_old[hh * DK_A:(hh + 1) * DK_A] * decay[hh * DK_A:(hh + 1) * DK_A]
             + upd[hh * DK_A:(hh + 1) * DK_A, hh * DV_A:(hh + 1) * DV_A] for hh in range(H_A)], axis=0)

    @pl.when(j == pl.num_programs(1) - 1)
    def _():
        sfin_ref[...] = s_ref[...]


def _gla(qa, ka, la, va, ra, s0, g_out, *, chunk, sub, nseq, tc, out_dtype):
    nb, t, _ = qa.shape
    grid = (nb // nseq, t // tc)
    tok = lambda width: pl.BlockSpec((nseq, tc, width), lambda b, j: (b, j, 0))
    st = pl.BlockSpec((nseq, HK, DV_A), lambda b, j: (b, 0, 0))
    return pl.pallas_call(
        functools.partial(_gla_kernel, chunk=chunk, sub=sub),
        out_shape=(jax.ShapeDtypeStruct((nb, t, HV), out_dtype),
                   jax.ShapeDtypeStruct((nb, HK, DV_A), F32)),
        grid=grid,
        in_specs=[tok(HK), tok(HK), tok(HK), tok(HV), tok(HV), st, _const_spec((1, DV_A))],
        out_specs=(tok(HV), st),
        scratch_shapes=[pltpu.VMEM((nseq, HK, DV_A), F32)],
        compiler_params=_cparams(("arbitrary", "arbitrary")), name="gla",
    )(qa, ka, la, va, ra, s0, g_out)


def _flash_kernel(q_ref, k_ref, vT_ref, g_ref, o_ref, m_ref, l_ref, acc_ref):
    qi = pl.program_id(1)
    tq = q_ref.shape[2]
    rows = H_B * tq
    heads = [slice(hh * tq, (hh + 1) * tq) for hh in range(H_B)]
    m_ref[...] = jnp.full(m_ref.shape, -jnp.inf, F32)
    l_ref[...] = jnp.zeros(l_ref.shape, F32)
    acc_ref[...] = jnp.zeros(acc_ref.shape, F32)

    def scores(j):
        keys = pl.ds(pl.multiple_of(j * tq, tq), tq)
        return jnp.concatenate([_mm_nt(k_ref[0, hh, keys, :], q_ref[0, hh]) for hh in range(H_B)],
                               axis=1)

    def values(j, pb):
        vt = vT_ref[0, j]
        return jnp.concatenate([_mm(vt[hh * V_B:(hh + 1) * V_B], pb[:, heads[hh]]) for hh in range(H_B)],
                               axis=1)

    def update(j, st, diagonal):
        if diagonal:
            t_k = lax.broadcasted_iota(jnp.int32, (tq, rows), 0)
            t_q = lax.broadcasted_iota(jnp.int32, (tq, rows), 1) % tq
            st = jnp.where(t_k <= t_q, st, -jnp.inf)
        m_prev = m_ref[...]
        m_new = jnp.maximum(m_prev, jnp.max(st, axis=0, keepdims=True))
        alpha = jnp.exp2(m_prev - m_new)
        p = jnp.exp2(st - m_new)
        l_ref[...] = alpha * l_ref[...] + jnp.sum(p, axis=0, keepdims=True)
        acc_ref[...] = alpha * acc_ref[...] + values(j, p.astype(BF16))
        m_ref[...] = m_new

    def full_block(j, st):
        st_next = scores(j + 1)
        update(j, st, False)
        return st_next

    update(qi, lax.fori_loop(0, qi, full_block, scores(0)), True)

    ob_t = acc_ref[...] / l_ref[...]
    for hh in range(H_B):
        y_t = ob_t[:, heads[hh]]
        y_t = y_t * lax.rsqrt(jnp.mean(y_t * y_t, axis=0, keepdims=True) + EPS) * g_ref[...]
        o_ref[0, :, hh * V_B:(hh + 1) * V_B] = y_t.T.astype(o_ref.dtype)


def _flash(qx, kx, v_t, g_col, *, tq):
    nb, _, t, _ = qx.shape
    rows = H_B * tq
    return pl.pallas_call(
        _flash_kernel,
        out_shape=jax.ShapeDtypeStruct((nb, t, H_B * V_B), BF16),
        grid=(nb, t // tq),
        in_specs=[pl.BlockSpec((1, H_B, tq, QX), lambda b, i: (b, 0, i, 0)),
                  pl.BlockSpec((1, H_B, t, QX), lambda b, i: (b, 0, 0, 0)),
                  pl.BlockSpec((1, t // tq, H_B * V_B, tq), lambda b, i: (b, 0, 0, 0)),
                  _const_spec((V_B, 1))],
        out_specs=pl.BlockSpec((1, tq, H_B * V_B), lambda b, i: (b, i, 0)),
        scratch_shapes=[pltpu.VMEM((1, rows), F32), pltpu.VMEM((1, rows), F32),
                        pltpu.VMEM((V_B, rows), F32)],
        compiler_params=_cparams(("arbitrary", "arbitrary")), name="flash",
    )(qx, kx, v_t, g_col)


def _decode_kernel(pt_ref, q_ref, knew_ref, wuv_ref, g_ref, lat_hbm, krT_hbm, o_ref,
                   lat_buf, kr_buf, s_scr, p_scr, sems, *, n_pages, td, ck):
    b = pl.program_id(0)
    slot = b % 2
    page = lat_hbm.shape[1]
    nck = (n_pages * page) // ck
    ppc = ck // page

    def page_copies(seq, sl, p):
        pg = pt_ref[seq * n_pages + p]
        rows = pl.ds(pl.multiple_of(p * page, page), page)
        return (pltpu.make_async_copy(lat_hbm.at[pg], lat_buf.at[sl, rows, :], sems.at[0, sl]),
                pltpu.make_async_copy(krT_hbm.at[pg], kr_buf.at[sl, p], sems.at[1, sl]))

    def fetch(seq, sl):
        def body(p, carry):
            for cp in page_copies(seq, sl, p):
                cp.start()
            return carry
        lax.fori_loop(0, n_pages, body, 0, unroll=8)

    @pl.when(b == 0)
    def _():
        fetch(0, 0)

    @pl.when(b + 1 < pl.num_programs(0))
    def _():
        fetch(b + 1, 1 - slot)

    pltpu.make_async_copy(lat_buf.at[slot], lat_buf.at[slot], sems.at[0, slot]).wait()
    pltpu.make_async_copy(kr_buf.at[slot], kr_buf.at[slot], sems.at[1, slot]).wait()

    q = q_ref[0].astype(BF16)
    ql, qr = q[:, 0:KV_LORA], q[:, KV_LORA:KV_LORA + QK_ROPE]

    def lat_chunk(c):
        return lat_buf[slot, c * ck:(c + 1) * ck, :].astype(BF16)

    def tree_sum(parts):
        while len(parts) > 1:
            parts = [parts[i] + parts[i + 1] if i + 1 < len(parts) else parts[i]
                     for i in range(0, len(parts), 2)]
        return parts[0]

    kn = jnp.concatenate([knew_ref[0], jnp.zeros((LANE - td, QCAT), F32)], axis=0).astype(BF16)
    s_new = _mm_nt(q, kn)
    t_q = lax.broadcasted_iota(jnp.int32, s_new.shape, 0) % td
    t_k = lax.broadcasted_iota(jnp.int32, s_new.shape, 1)
    s_new = jnp.where(t_k <= t_q, s_new, -jnp.inf)
    m = jnp.max(s_new, axis=-1, keepdims=True)
    p_new = jnp.exp2(s_new - m)
    l = jnp.sum(p_new, axis=-1, keepdims=True)
    acc = _mm(p_new.astype(BF16), kn[:, 0:KV_LORA])

    cps = nck // DECODE_SEGMENTS
    for seg in range(DECODE_SEGMENTS):
        chunks = range(seg * cps, (seg + 1) * cps)
        for c in chunks:
            s_rope = jnp.concatenate(
                [_mm(qr, kr_buf[slot, c * ppc + i].astype(BF16)) for i in range(ppc)], axis=1)
            s_scr[c] = _mm_nt(ql, lat_chunk(c)) + s_rope
        s_seg = s_scr[seg * cps:(seg + 1) * cps]
        m_new = jnp.maximum(m, jnp.max(jnp.max(s_seg, axis=0), axis=-1, keepdims=True))
        alpha = jnp.exp2(m - m_new)
        p_seg = jnp.exp2(s_seg - m_new[None])
        l = alpha * l + jnp.sum(jnp.sum(p_seg, axis=0), axis=-1, keepdims=True)
        p_scr[seg * cps:(seg + 1) * cps] = p_seg.astype(BF16)
        acc = alpha * acc + tree_sum([_mm(p_scr[c], lat_chunk(c)) for c in chunks])
        m = m_new
    o_lat = (acc / l).astype(BF16)
    ob = _mm(o_lat, wuv_ref[...])
    for hh in range(H_B):
        blk = ob[hh * td:(hh + 1) * td, hh * V_B:(hh + 1) * V_B]
        o_ref[0, :, hh * V_B:(hh + 1) * V_B] = _rms(blk, g_ref[...]).astype(o_ref.dtype)


def _decode(qcat, knew, wuv_all, g_out, cache_lat, cache_krT, page_table, *, ck):
    nb, _, td, _ = qcat.shape
    qcat = qcat.reshape(nb, H_B * td, QCAT)
    n_pages = page_table.shape[1]
    page = cache_lat.shape[1]
    past = n_pages * page
    rows = H_B * td
    grid_spec = pltpu.PrefetchScalarGridSpec(
        num_scalar_prefetch=1,
        grid=(nb,),
        in_specs=[pl.BlockSpec((1, rows, QCAT), lambda b, pt: (b, 0, 0)),
                  pl.BlockSpec((1, td, QCAT), lambda b, pt: (b, 0, 0)),
                  pl.BlockSpec(wuv_all.shape, lambda b, pt: (0, 0)),
                  pl.BlockSpec((1, V_B), lambda b, pt: (0, 0)),
                  pl.BlockSpec(memory_space=pl.ANY),
                  pl.BlockSpec(memory_space=pl.ANY)],
        out_specs=pl.BlockSpec((1, td, H_B * V_B), lambda b, pt: (b, 0, 0)),
        scratch_shapes=[pltpu.VMEM((2, past, KV_LORA), F32),
                        pltpu.VMEM((2, n_pages, QK_ROPE, page), F32),
                        pltpu.VMEM((past // ck, rows, ck), F32),
                        pltpu.VMEM((past // ck, rows, ck), BF16),
                        pltpu.SemaphoreType.DMA((2, 2))],
    )
    return pl.pallas_call(
        functools.partial(_decode_kernel, n_pages=n_pages, td=td, ck=ck),
        out_shape=jax.ShapeDtypeStruct((nb, td, H_B * V_B), F32),
        grid_spec=grid_spec,
        compiler_params=_cparams(("arbitrary",)), name="decode",
    )(page_table.reshape(-1), qcat, knew, wuv_all, g_out, cache_lat, cache_krT)


def _outproj_kernel(oa_ref, ob_ref, x_ref, mod_ref, wo_ref, gpost_ref, gpre_ref, x1_ref, h2_ref):
    bb, tt, d = x_ref.shape
    m = bb * tt
    mod = mod_ref[...]
    gt1, sh2, sc2 = mod[:, :, 2 * d:3 * d], mod[:, :, 3 * d:4 * d], mod[:, :, 4 * d:5 * d]
    mix = (_mm(oa_ref[...].reshape(m, HV).astype(BF16), wo_ref[0:HV, :])
           + _mm(ob_ref[...].reshape(m, H_B * V_B).astype(BF16), wo_ref[HV:HV + H_B * V_B, :]))
    x1 = x_ref[...] + gt1 * _rms(mix, gpost_ref[...]).reshape(bb, tt, d)
    x1_ref[...] = x1
    h2_ref[...] = (_rms(x1, gpre_ref[...]) * (1.0 + sc2) + sh2).astype(h2_ref.dtype)


def _outproj(oa, ob, x, mod3, w, *, bb, tt, act_dtype):
    nb, t, d = x.shape
    grid = (nb // bb, t // tt)
    tok = lambda width: pl.BlockSpec((bb, tt, width), lambda i, j: (i, j, 0))
    return pl.pallas_call(
        _outproj_kernel,
        out_shape=(jax.ShapeDtypeStruct((nb, t, d), F32), jax.ShapeDtypeStruct((nb, t, d), act_dtype)),
        grid=grid,
        in_specs=[tok(HV), tok(H_B * V_B), tok(d),
                  pl.BlockSpec((bb, 1, mod3.shape[2]), lambda i, j: (i, 0, 0)),
                  _const_spec(w["wo"].shape), _const_spec((1, d)), _const_spec((1, d))],
        out_specs=(tok(d), tok(d)),
        compiler_params=_cparams(("arbitrary", "arbitrary")), name="outproj",
    )(oa, ob, x, mod3, w["wo"], w["g_post_mix"], w["g_pre_ffn"])


def _gelu_tanh(x):
    c = math.sqrt(2.0 / math.pi)
    hx = 0.5 * x
    return hx + hx * jnp.tanh(x * (c + (c * 0.044715) * (x * x)))


def _ffn_kernel(h2_ref, x1_ref, mod_ref, cst_ref, wup_ref, wc_ref, bc_ref, wdn_ref, gpost_ref,
                y_ref, cnew_ref, carry_ref):
    bb, tt, d = x1_ref.shape
    m = bb * tt
    fc = FFN_CHUNK
    nc = wdn_ref.shape[0] // fc

    @pl.when(pl.program_id(1) == 0)
    def _():
        carry_ref[...] = cst_ref[...]

    assert bb == 1 or tt == SUBLANE
    ng = m // SUBLANE
    h2 = h2_ref[...].reshape(m, d).astype(BF16)
    sub = lax.broadcasted_iota(jnp.int32, (1, SUBLANE, fc), 1)

    def up_conv(k):
        cols = slice(k * fc, (k + 1) * fc)
        u = _mm(h2, wup_ref[:, cols])
        prev = carry_ref[:, :, cols]
        tail = u.reshape(bb, tt, fc)[:, tt - (CONV_W - 1):tt]
        carry_ref[:, :, cols] = tail
        cnew_ref[:, :, cols] = tail
        g3 = u.reshape(ng, SUBLANE, fc)
        r1, r2 = pltpu.roll(g3, 1, axis=1), pltpu.roll(g3, 2, axis=1)
        p1 = jnp.broadcast_to(prev[:, 1:2], (bb, SUBLANE, fc))
        p2 = jnp.where(sub == 0, jnp.broadcast_to(prev[:, 0:1], (bb, SUBLANE, fc)), p1)
        if bb == 1:
            p1 = jnp.concatenate([p1, r1[:ng - 1]], axis=0)
            p2 = jnp.concatenate([p2, r2[:ng - 1]], axis=0)
        u1 = jnp.where(sub < 1, p1, r1).reshape(m, fc)
        u2 = jnp.where(sub < 2, p2, r2).reshape(m, fc)
        wc = wc_ref[:, cols]
        return bc_ref[:, cols] + wc[2:3] * u + wc[1:2] * u1 + wc[0:1] * u2

    acc = jnp.zeros((m, d), F32)
    ready = [(up_conv(c), up_conv(nc + c)) for c in range(min(FFN_AHEAD, nc))]
    for c in range(nc):
        if c + FFN_AHEAD < nc:
            ready.append((up_conv(c + FFN_AHEAD), up_conv(nc + c + FFN_AHEAD)))
        val, gate = ready.pop(0)
        act = (val * _gelu_tanh(gate)).astype(BF16)
        acc = acc + _mm(act, wdn_ref[c * fc:(c + 1) * fc, :])
    gt2 = mod_ref[...][:, :, 5 * d:6 * d]
    y_ref[...] = x1_ref[...] + gt2 * _rms(acc, gpost_ref[...]).reshape(bb, tt, d)


def _ffn(h2, x1, mod3, cstate, w, *, bb, tt):
    nb, t, d = x1.shape
    f2 = cstate.shape[2]
    grid = (nb // bb, t // tt)
    tok = lambda width: pl.BlockSpec((bb, tt, width), lambda i, j: (i, j, 0))
    cs_spec = pl.BlockSpec((bb, CONV_W - 1, f2), lambda i, j: (i, 0, 0))
    return pl.pallas_call(
        _ffn_kernel,
        out_shape=(jax.ShapeDtypeStruct((nb, t, d), F32),
                   jax.ShapeDtypeStruct((nb, CONV_W - 1, f2), F32)),
        grid=grid,
        in_specs=[tok(d), tok(d),
                  pl.BlockSpec((bb, 1, mod3.shape[2]), lambda i, j: (i, 0, 0)),
                  cs_spec,
                  _const_spec(w["wup"].shape), _const_spec(w["wconv"].shape), _const_spec(w["bconv"].shape),
                  _const_spec(w["wdn"].shape), _const_spec((1, d))],
        out_specs=(tok(d), cs_spec),
        scratch_shapes=[pltpu.VMEM((bb, CONV_W - 1, f2), F32)],
        compiler_params=_cparams(("arbitrary", "arbitrary")), name="ffn",
    )(h2, x1, mod3, cstate, w["wup"], w["wconv"], w["bconv"], w["wdn"], w["g_post_ffn"])


def _rot_cols(wr):
    half = wr.shape[-1] // 2
    return jnp.concatenate([-wr[..., half:], wr[..., :half]], axis=-1)


def _prep_weights(l, w_in, w_gla_a2, b_gla_a, g_gla_out, g_mla_q, g_mla_kv, w_mla_uq, w_mla_uk, w_mla_uv,
                  g_mla_out, w_o, w_ffn_up, w_ffn_conv, b_ffn_conv, w_ffn_down, g_pre_mix, g_post_mix,
                  g_pre_ffn, g_post_ffn):
    wi = w_in[l]
    d = wi.shape[0]
    o_gr = 2 * HK + 2 * HV
    o_cq = o_gr + GATE_RANK
    o_ckv = o_cq + Q_LORA
    o_kr = o_ckv + KV_LORA
    kr = wi[:, o_kr:o_kr + QK_ROPE]
    win = jnp.concatenate(
        [wi[:, 0:o_gr], wi[:, o_cq:o_ckv], wi[:, o_ckv:o_kr], kr, _rot_cols(kr),
         wi[:, o_gr:o_cq], jnp.zeros((d, LANE - GATE_RANK), wi.dtype)], axis=1).astype(BF16)
    wa2 = jnp.concatenate([w_gla_a2[l], jnp.zeros((LANE - GATE_RANK, HK), F32)], axis=0).astype(BF16)
    uq = w_mla_uq[l].reshape(Q_LORA, H_B, QK_NOPE + QK_ROPE)
    uq_rope = uq[:, :, QK_NOPE:]
    wuq = jnp.concatenate(
        [uq[:, :, :QK_NOPE].reshape(Q_LORA, H_B * QK_NOPE),
         jnp.concatenate([uq_rope, _rot_cols(uq_rope)], axis=-1).reshape(Q_LORA, H_B * 2 * QK_ROPE)],
        axis=1).astype(BF16)
    return dict(
        win=win, wa2=wa2, ba=b_gla_a[l].reshape(1, HK),
        g_gla_out=g_gla_out[l].reshape(1, DV_A),
        g_mla_q=g_mla_q[l].reshape(1, Q_LORA), g_mla_kv=g_mla_kv[l].reshape(1, KV_LORA),
        wuq=wuq, wukT=jnp.transpose(w_mla_uk[l], (1, 2, 0)).astype(BF16),
        wukx=w_mla_uk[l].reshape(KV_LORA, H_B * QK_NOPE).astype(BF16),
        wuvT=w_mla_uv[l].reshape(KV_LORA, H_B * V_B).T.astype(BF16),
        g_mla_outT=g_mla_out[l].reshape(V_B, 1),
        wuv_all=w_mla_uv[l].reshape(KV_LORA, H_B * V_B).astype(BF16),
        g_mla_out=g_mla_out[l].reshape(1, V_B),
        wo=w_o[l].astype(BF16),
        wup=w_ffn_up[l].astype(BF16), wconv=w_ffn_conv[l], bconv=b_ffn_conv[l].reshape(1, -1),
        wdn=w_ffn_down[l].astype(BF16),
        g_pre_mix=g_pre_mix[l].reshape(1, d), g_post_mix=g_post_mix[l].reshape(1, d),
        g_pre_ffn=g_pre_ffn[l].reshape(1, d), g_post_ffn=g_post_ffn[l].reshape(1, d),
    )


def _rope_table(pos):
    inv = ROPE_BASE ** (-jnp.arange(0, QK_ROPE, 2, dtype=F32) / QK_ROPE)
    ang = pos.astype(F32)[:, None] * inv[None, :]
    cos, sin = jnp.cos(ang), jnp.sin(ang)
    return jnp.concatenate([cos, cos, sin, sin], axis=-1)


def _token_tile(t, target):
    tt = min(t, target)
    while t % tt:
        tt //= 2
    return tt


def _layer(x, mod3, cs, w, gla_s0, conv_state, *, per_seq, attend):
    nb, t, d = x.shape
    if per_seq:
        bb, tt = nb, t
    else:
        bb, tt = 1, _token_tile(t, 512)
    act_dtype = F32 if per_seq else BF16
    tkb = _token_tile(bb * tt, ATTN_TILE)
    qa, ka, va, ra, la, *mla, lat, kr = _inproj(
        x, mod3, cs, w, bb=bb, tt=tt, tkb=tkb, act_dtype=act_dtype, expanded=not per_seq)

    if per_seq:
        o_a, s_fin = _gla(qa, ka, la, va, ra, gla_s0, w["g_gla_out"], chunk=GLA_CHUNK_SHORT,
                          sub=GLA_CHUNK_SHORT, nseq=_token_tile(nb, 8), tc=t, out_dtype=act_dtype)
    else:
        o_a, s_fin = _gla(qa, ka, la, va, ra, gla_s0, w["g_gla_out"], chunk=LANE, sub=LANE // 4,
                          nseq=1, tc=_token_tile(t, 512), out_dtype=act_dtype)

    if not per_seq:
        kr = jnp.swapaxes(kr, 1, 2)
    o_b = attend(*mla)
    x1, h2 = _outproj(o_a, o_b, x, mod3, w, bb=bb, tt=tt, act_dtype=act_dtype)
    ffn_tt = tt if per_seq else _token_tile(t, FFN_ROWS)
    y, c_new = _ffn(h2, x1, mod3, conv_state, w, bb=bb, tt=ffn_tt)
    return y, lat, kr, s_fin, c_new


def kernel(x_prompt, x_sample, c_prompt, c_sample, cache_latent, cache_krope, state_gla, state_conv, page_table, w_ada, b_ada, g_pre_mix, g_post_mix, g_pre_ffn, g_post_ffn, w_in, w_gla_a2, b_gla_a, g_gla_out, g_mla_q, g_mla_kv, w_mla_uq, w_mla_uk, w_mla_uv, g_mla_out, w_o, w_ffn_up, w_ffn_conv, b_ffn_conv, w_ffn_down):
    depth = w_ada.shape[0]
    nbp, tp, d = x_prompt.shape
    nbs, ts, _ = x_sample.shape
    n_pages, page = page_table.shape[1], cache_latent.shape[2]
    past_len = n_pages * page
    f2 = w_ffn_up.shape[2]
    cs_p = _rope_table(jnp.arange(tp, dtype=jnp.int32))
    cs_s = _rope_table(past_len + jnp.arange(ts, dtype=jnp.int32))

    hp, hs = x_prompt, x_sample
    outs = [[] for _ in range(8)]
    for l in range(depth):
        w = _prep_weights(l, w_in, w_gla_a2, b_gla_a, g_gla_out, g_mla_q, g_mla_kv, w_mla_uq, w_mla_uk,
                          w_mla_uv, g_mla_out, w_o, w_ffn_up, w_ffn_conv, b_ffn_conv, w_ffn_down,
                          g_pre_mix, g_post_mix, g_pre_ffn, g_post_ffn)
        mod = _ada(jnp.concatenate([c_prompt, c_sample], axis=0), w_ada[l], b_ada[l])
        mod3 = mod.reshape(nbp + nbs, 1, mod.shape[1])

        def attend_prompt(qx, kx, v_t):
            return _flash(qx, kx, v_t, w["g_mla_outT"], tq=_token_tile(tp, ATTN_TILE))

        def attend_sample(qcat, kcat):
            return _decode(qcat, kcat, w["wuv_all"], w["g_mla_out"], cache_latent[l],
                           jnp.swapaxes(cache_krope[l], 1, 2), page_table, ck=8 * page)

        hp, a1, a2, a3, a4 = _layer(
            hp, mod3[:nbp], cs_p, w, jnp.zeros((nbp, HK, DV_A), F32),
            jnp.zeros((nbp, CONV_W - 1, f2), F32), per_seq=False, attend=attend_prompt)
        hs, b1, b2, b3, b4 = _layer(
            hs, mod3[nbp:], cs_s, w, state_gla[l].reshape(nbs, HK, DV_A), state_conv[l],
            per_seq=True, attend=attend_sample)
        for lst, val in zip(outs, (a1, a2, a3.reshape(nbp, H_A, DK_A, DV_A), a4,
                                   b1, b2, b3.reshape(nbs, H_A, DK_A, DV_A), b4)):
            lst.append(val)
    return (hp, hs) + tuple(jnp.stack(o) for o in outs)
```

```python
import functools
import math

import jax
import jax.numpy as jnp
import numpy as np
from jax import lax
from jax.experimental import pallas as pl
from jax.experimental.pallas import tpu as pltpu

F32 = jnp.float32
BF16 = jnp.bfloat16

EPS = 1e-6
H_A, DK_A, DV_A = 4, 64, 128
GATE_RANK = 16
GATE_TAU = 16.0
H_B = 4
Q_LORA, KV_LORA = 384, 256
QK_NOPE, QK_ROPE, V_B = 128, 64, 128
ROPE_BASE = 10000.0
CONV_W = 3

HK = H_A * DK_A
HV = H_A * DV_A
QCAT = KV_LORA + 2 * QK_ROPE
QX = QK_NOPE + 2 * QK_ROPE
LANE = 128
SUBLANE = 8

_O_QA, _O_KA, _O_VA, _O_RA = 0, HK, 2 * HK, 2 * HK + HV
_O_CQ = 2 * HK + 2 * HV
_O_CKV = _O_CQ + Q_LORA
_O_KRR = _O_CKV + KV_LORA
_O_GR = _O_KRR + 2 * QK_ROPE
IN_EXT = _O_GR + LANE

QK_SCALE_LOG2E = (QK_NOPE + QK_ROPE) ** -0.5 * math.log2(math.e)

GLA_CHUNK_SHORT = 16
ATTN_TILE = 256
DECODE_SEGMENTS = 4
FFN_CHUNK = 256
FFN_ROWS = 256
FFN_AHEAD = 4
VMEM_LIMIT = 56 * 1024 * 1024
GLA_EXP_CLAMP = 80.0


def _mm(a, b):
    return jnp.dot(a, b, preferred_element_type=F32)


def _mm_nt(a, b):
    return lax.dot_general(a, b, (((1,), (1,)), ((), ())), preferred_element_type=F32)


def _rms(x, g):
    return x * lax.rsqrt(jnp.mean(x * x, axis=-1, keepdims=True) + EPS) * g


def _silu(x):
    return x / (1.0 + jnp.exp(-x))


def _log_sigmoid(x):
    return -(jnp.maximum(-x, 0.0) + jnp.log(1.0 + jnp.exp(-jnp.abs(x))))


def _cparams(sem):
    return pltpu.CompilerParams(dimension_semantics=sem, vmem_limit_bytes=VMEM_LIMIT)


def _const_spec(shape):
    nd = len(shape)
    return pl.BlockSpec(shape, lambda *_: (0,) * nd)


def _ada_kernel(c_ref, w_ref, b_ref, o_ref):
    s = _silu(c_ref[...]).astype(BF16)
    o_ref[...] = _mm(s, w_ref[...].astype(BF16)) + b_ref[...]


def _ada(c_all, w_ada, b_ada):
    n, d = c_all.shape
    nout = w_ada.shape[1]
    tn = d
    return pl.pallas_call(
        _ada_kernel,
        out_shape=jax.ShapeDtypeStruct((n, nout), F32),
        grid=(nout // tn,),
        in_specs=[pl.BlockSpec((n, d), lambda j: (0, 0)),
                  pl.BlockSpec((d, tn), lambda j: (0, j)),
                  pl.BlockSpec((1, tn), lambda j: (0, j))],
        out_specs=pl.BlockSpec((n, tn), lambda j: (0, j)),
        compiler_params=_cparams(("arbitrary",)),
        name="ada",
    )(c_all, w_ada, b_ada.reshape(1, nout))


def _rope_sum(a3, cs):
    bb, tt, w = a3.shape
    a = (a3 * cs[None]).reshape(bb * tt, w)
    return a + pltpu.roll(a, QK_ROPE, axis=1)


def _inproj_kernel(x_ref, mod_ref, gpre_ref, win_ref, wa2_ref, ba_ref, gq_ref, gkv_ref, wuq_ref, cs_ref,
                   *rest, expanded):
    if expanded:
        wukx_ref, wuvT_ref = rest[:2]
        qa_ref, ka_ref, va_ref, ra_ref, la_ref, qx_ref, kx_ref, vT_ref, lat_ref, kr_ref = rest[2:]
    else:
        (wukT_ref,) = rest[:1]
        qa_ref, ka_ref, va_ref, ra_ref, la_ref, qcat_ref, kcat_ref, lat_ref, kr_ref = rest[1:]
    bb, tt, d = x_ref.shape
    m = bb * tt
    x = x_ref[...]
    mod = mod_ref[...]
    sh1, sc1 = mod[:, :, 0:d], mod[:, :, d:2 * d]
    h = _rms(x, gpre_ref[...]) * (1.0 + sc1) + sh1
    h = h.reshape(m, d).astype(BF16)
    z = _mm(h, win_ref[...])

    qa_ref[...] = (z[:, _O_QA:_O_QA + HK] * (DK_A ** -0.5)).reshape(bb, tt, HK)
    ka_ref[...] = z[:, _O_KA:_O_KA + HK].reshape(bb, tt, HK)
    va_ref[...] = z[:, _O_VA:_O_VA + HV].astype(va_ref.dtype).reshape(bb, tt, HV)
    ra_ref[...] = z[:, _O_RA:_O_RA + HV].astype(ra_ref.dtype).reshape(bb, tt, HV)
    gr = z[:, _O_GR:_O_GR + LANE].astype(BF16)
    la = _log_sigmoid(_mm(gr, wa2_ref[...]) + ba_ref[...]) * (1.0 / GATE_TAU)
    la_ref[...] = la.reshape(bb, tt, HK)

    cs = cs_ref[...]
    ckv = _rms(z[:, _O_CKV:_O_CKV + KV_LORA], gkv_ref[...])
    lat_ref[...] = ckv.reshape(bb, tt, KV_LORA)
    krs = _rope_sum(z[:, _O_KRR:_O_KRR + 2 * QK_ROPE].reshape(bb, tt, 2 * QK_ROPE), cs)
    if len(kr_ref.shape) == 2:
        kr_ref[...] = krs.T[0:QK_ROPE, :]
    else:
        kr_ref[...] = krs[:, 0:QK_ROPE].reshape(bb, tt, QK_ROPE)
    lane = lax.broadcasted_iota(jnp.int32, (m, 2 * QK_ROPE), 1)
    kr0 = jnp.where(lane < QK_ROPE, krs, 0.0)
    cq = _rms(z[:, _O_CQ:_O_CQ + Q_LORA], gq_ref[...]).astype(BF16)
    qb = _mm(cq, wuq_ref[...])

    def q_rope(hh):
        o = H_B * QK_NOPE + hh * 2 * QK_ROPE
        return _rope_sum(qb[:, o:o + 2 * QK_ROPE].reshape(bb, tt, 2 * QK_ROPE), cs) * QK_SCALE_LOG2E

    if expanded:
        ckv_b = ckv.astype(BF16)
        kn = _mm(ckv_b, wukx_ref[...])
        vt = _mm_nt(wuvT_ref[...], ckv_b)
        tkb = vT_ref.shape[2]
        for c in range(vT_ref.shape[0]):
            vT_ref[c] = vt[:, c * tkb:(c + 1) * tkb].astype(vT_ref.dtype)
        for hh in range(H_B):
            nope = slice(hh * QK_NOPE, (hh + 1) * QK_NOPE)
            kx_ref[:, hh, :, 0:QK_NOPE] = kn[:, nope].astype(kx_ref.dtype).reshape(bb, tt, QK_NOPE)
            kx_ref[:, hh, :, QK_NOPE:QX] = kr0.astype(kx_ref.dtype).reshape(bb, tt, 2 * QK_ROPE)
            qx_ref[:, hh, :, 0:QK_NOPE] = (qb[:, nope] * QK_SCALE_LOG2E).astype(qx_ref.dtype).reshape(bb, tt, QK_NOPE)
            qx_ref[:, hh, :, QK_NOPE:QX] = q_rope(hh).astype(qx_ref.dtype).reshape(bb, tt, 2 * QK_ROPE)
    else:
        kcat_ref[:, :, 0:KV_LORA] = ckv.astype(kcat_ref.dtype).reshape(bb, tt, KV_LORA)
        kcat_ref[:, :, KV_LORA:QCAT] = kr0.astype(kcat_ref.dtype).reshape(bb, tt, 2 * QK_ROPE)
        for hh in range(H_B):
            qn = qb[:, hh * QK_NOPE:(hh + 1) * QK_NOPE].astype(BF16)
            qlat = _mm(qn, wukT_ref[hh]) * QK_SCALE_LOG2E
            qcat_ref[:, hh, :, 0:KV_LORA] = qlat.astype(qcat_ref.dtype).reshape(bb, tt, KV_LORA)
            qcat_ref[:, hh, :, KV_LORA:QCAT] = q_rope(hh).astype(qcat_ref.dtype).reshape(bb, tt, 2 * QK_ROPE)


def _inproj(x, mod3, cs, w, *, bb, tt, tkb, act_dtype, expanded):
    nb, t, d = x.shape
    m = bb * tt
    grid = (nb // bb, t // tt)
    tok = lambda width: pl.BlockSpec((bb, tt, width), lambda i, j: (i, j, 0))
    heads = lambda width: pl.BlockSpec((bb, H_B, tt, width), lambda i, j: (i, 0, j, 0))
    if expanded:
        mla_w = (w["wukx"], w["wuvT"])
        mla_shape = (jax.ShapeDtypeStruct((nb, H_B, t, QX), act_dtype),
                     jax.ShapeDtypeStruct((nb, H_B, t, QX), act_dtype),
                     jax.ShapeDtypeStruct((nb // bb, (t // tt) * (m // tkb), H_B * V_B, tkb), BF16))
        mla_specs = (heads(QX), heads(QX),
                     pl.BlockSpec((None, m // tkb, H_B * V_B, tkb), lambda i, j: (i, j, 0, 0)))
    else:
        mla_w = (w["wukT"],)
        mla_shape = (jax.ShapeDtypeStruct((nb, H_B, t, QCAT), act_dtype),
                     jax.ShapeDtypeStruct((nb, t, QCAT), act_dtype))
        mla_specs = (heads(QCAT), tok(QCAT))
    kr_shape = jax.ShapeDtypeStruct((nb, QK_ROPE, t) if bb == 1 else (nb, t, QK_ROPE), F32)
    kr_spec = pl.BlockSpec((None, QK_ROPE, tt), lambda i, j: (i, 0, j)) if bb == 1 else tok(QK_ROPE)
    out_shape = (
        jax.ShapeDtypeStruct((nb, t, HK), F32),
        jax.ShapeDtypeStruct((nb, t, HK), F32),
        jax.ShapeDtypeStruct((nb, t, HV), act_dtype),
        jax.ShapeDtypeStruct((nb, t, HV), act_dtype),
        jax.ShapeDtypeStruct((nb, t, HK), F32),
    ) + mla_shape + (jax.ShapeDtypeStruct((nb, t, KV_LORA), F32), kr_shape)
    out_specs = (tok(HK), tok(HK), tok(HV), tok(HV), tok(HK)) + mla_specs + (tok(KV_LORA), kr_spec)
    in_specs = [
        tok(d),
        pl.BlockSpec((bb, 1, mod3.shape[2]), lambda i, j: (i, 0, 0)),
        _const_spec((1, d)),
        _const_spec(w["win"].shape),
        _const_spec(w["wa2"].shape), _const_spec((1, HK)),
        _const_spec((1, Q_LORA)), _const_spec((1, KV_LORA)),
        _const_spec(w["wuq"].shape),
        pl.BlockSpec((tt, 2 * QK_ROPE), lambda i, j: (j, 0)),
    ] + [_const_spec(a.shape) for a in mla_w]
    return pl.pallas_call(
        functools.partial(_inproj_kernel, expanded=expanded),
        out_shape=out_shape, grid=grid, in_specs=in_specs, out_specs=out_specs,
        compiler_params=_cparams(("arbitrary", "arbitrary")), name="inproj",
    )(x, mod3, w["g_pre_mix"], w["win"], w["wa2"], w["ba"], w["g_mla_q"], w["g_mla_kv"], w["wuq"], cs, *mla_w)


def _gla_kernel(q_ref, k_ref, la_ref, v_ref, r_ref, s0_ref, g_ref,
                o_ref, sfin_ref, s_ref, *, chunk, sub):
    j = pl.program_id(1)
    c_, r_ = chunk, sub
    nsub = c_ // r_
    nseq, tc = q_ref.shape[0], q_ref.shape[1]
    t_in = min(tc, c_)

    @pl.when(j == 0)
    def _():
        s_ref[...] = s0_ref[...]

    row = lax.broadcasted_iota(jnp.int32, (2 * c_, c_), 0)
    col = lax.broadcasted_iota(jnp.int32, (2 * c_, c_), 1)
    bound = jnp.where(row < c_, row + 1, ((row - c_) // r_) * r_)
    lmat = jnp.where(col < bound, 1.0, 0.0).astype(BF16)
    lane = lax.broadcasted_iota(jnp.int32, (1, HK), 1)
    hmask = [(lane >= hh * DK_A) & (lane < (hh + 1) * DK_A) for hh in range(H_A)]
    g_out = g_ref[...]

    def stack_heads(a):
        return jnp.concatenate([jnp.where(hmask[hh], a, 0.0) for hh in range(H_A)], axis=0).astype(BF16)

    def chunk_rows(ref, sq, sl):
        a = ref[sq, sl, :].astype(F32)
        if t_in == c_:
            return a
        return jnp.concatenate([a, jnp.zeros((c_ - t_in, a.shape[1]), F32)], axis=0)

    for c, sq in [(c, sq) for c in range(max(tc // c_, 1)) for sq in range(nseq)]:
        sl = slice(c * c_, c * c_ + t_in)
        g = chunk_rows(la_ref, sq, sl)
        g1 = g.astype(BF16)
        g2 = (g - g1.astype(F32)).astype(BF16)
        bm = _mm(lmat, g1) + _mm(lmat, g2)
        b, mrow = bm[:c_], bm[c_:]
        q = chunk_rows(q_ref, sq, sl)
        k = chunk_rows(k_ref, sq, sl)
        v = chunk_rows(v_ref, sq, sl).astype(BF16)
        s_old = s_ref[sq]

        o_inter = _mm(stack_heads(q * jnp.exp(b)), s_old.astype(BF16))

        qt = q * jnp.exp(b - mrow)
        o_sub = [[] for _ in range(H_A)]
        for i in range(nsub):
            n_k = (i + 1) * r_
            m_i = mrow[i * r_:i * r_ + 1, :]
            kt = (k[:n_k] * jnp.exp(jnp.minimum(m_i - b[:n_k], GLA_EXP_CLAMP))).astype(BF16)
            att = _mm_nt(stack_heads(qt[i * r_:(i + 1) * r_]), kt)
            rr = lax.broadcasted_iota(jnp.int32, (H_A * r_, n_k), 0) % r_
            cc = lax.broadcasted_iota(jnp.int32, (H_A * r_, n_k), 1)
            att = jnp.where(cc <= rr + i * r_, att, 0.0).astype(BF16)
            pv = _mm(att, v[:n_k])
            for hh in range(H_A):
                o_sub[hh].append(pv[hh * r_:(hh + 1) * r_, hh * DV_A:(hh + 1) * DV_A])
        for hh in range(H_A):
            o_h = (o_inter[hh * c_:(hh + 1) * c_] + jnp.concatenate(o_sub[hh], axis=0))[:t_in]
            gate = r_ref[sq, sl, hh * DV_A:(hh + 1) * DV_A].astype(F32)
            o_ref[sq, sl, hh * DV_A:(hh + 1) * DV_A] = (_rms(o_h, g_out) * _silu(gate)).astype(o_ref.dtype)

        bl = b[c_ - 1:c_, :]
        klT = (k * jnp.exp(bl - b)).T.astype(BF16)
        upd = _mm(klT, v)
        decay = jnp.exp(jnp.broadcast_to(bl, (DV_A, HK)).T)
        s_ref[sq] = jnp.concatenate(
            [s_old[hh * DK_A:(hh + 1) * DK_A] * decay[hh * DK_A:(hh + 1) * DK_A]
             + upd[hh * DK_A:(hh + 1) * DK_A, hh * DV_A:(hh + 1) * DV_A] for hh in range(H_A)], axis=0)

    @pl.when(j == pl.num_programs(1) - 1)
    def _():
        sfin_ref[...] = s_ref[...]


def _gla(qa, ka, la, va, ra, s0, g_out, *, chunk, sub, nseq, tc, out_dtype):
    nb, t, _ = qa.shape
    grid = (nb // nseq, t // tc)
    tok = lambda width: pl.BlockSpec((nseq, tc, width), lambda b, j: (b, j, 0))
    st = pl.BlockSpec((nseq, HK, DV_A), lambda b, j: (b, 0, 0))
    return pl.pallas_call(
        functools.partial(_gla_kernel, chunk=chunk, sub=sub),
        out_shape=(jax.ShapeDtypeStruct((nb, t, HV), out_dtype),
                   jax.ShapeDtypeStruct((nb, HK, DV_A), F32)),
        grid=grid,
        in_specs=[tok(HK), tok(HK), tok(HK), tok(HV), tok(HV), st, _const_spec((1, DV_A))],
        out_specs=(tok(HV), st),
        scratch_shapes=[pltpu.VMEM((nseq, HK, DV_A), F32)],
        compiler_params=_cparams(("arbitrary", "arbitrary")), name="gla",
    )(qa, ka, la, va, ra, s0, g_out)


def _flash_kernel(q_ref, k_ref, vT_ref, g_ref, o_ref, m_ref, l_ref, acc_ref):
    qi = pl.program_id(1)
    tq = q_ref.shape[2]
    rows = H_B * tq
    heads = [slice(hh * tq, (hh + 1) * tq) for hh in range(H_B)]
    m_ref[...] = jnp.full(m_ref.shape, -jnp.inf, F32)
    l_ref[...] = jnp.zeros(l_ref.shape, F32)
    acc_ref[...] = jnp.zeros(acc_ref.shape, F32)

    def scores(j):
        keys = pl.ds(pl.multiple_of(j * tq, tq), tq)
        return jnp.concatenate([_mm_nt(k_ref[0, hh, keys, :], q_ref[0, hh]) for hh in range(H_B)],
                               axis=1)

    def values(j, pb):
        vt = vT_ref[0, j]
        return jnp.concatenate([_mm(vt[hh * V_B:(hh + 1) * V_B], pb[:, heads[hh]]) for hh in range(H_B)],
                               axis=1)

    def update(j, st, diagonal):
        if diagonal:
            t_k = lax.broadcasted_iota(jnp.int32, (tq, rows), 0)
            t_q = lax.broadcasted_iota(jnp.int32, (tq, rows), 1) % tq
            st = jnp.where(t_k <= t_q, st, -jnp.inf)
        m_prev = m_ref[...]
        m_new = jnp.maximum(m_prev, jnp.max(st, axis=0, keepdims=True))
        alpha = jnp.exp2(m_prev - m_new)
        p = jnp.exp2(st - m_new)
        l_ref[...] = alpha * l_ref[...] + jnp.sum(p, axis=0, keepdims=True)
        acc_ref[...] = alpha * acc_ref[...] + values(j, p.astype(BF16))
        m_ref[...] = m_new

    def full_block(j, st):
        st_next = scores(j + 1)
        update(j, st, False)
        return st_next

    update(qi, lax.fori_loop(0, qi, full_block, scores(0)), True)

    ob_t = acc_ref[...] / l_ref[...]
    for hh in range(H_B):
        y_t = ob_t[:, heads[hh]]
        y_t = y_t * lax.rsqrt(jnp.mean(y_t * y_t, axis=0, keepdims=True) + EPS) * g_ref[...]
        o_ref[0, :, hh * V_B:(hh + 1) * V_B] = y_t.T.astype(o_ref.dtype)


def _flash(qx, kx, v_t, g_col, *, tq):
    nb, _, t, _ = qx.shape
    rows = H_B * tq
    return pl.pallas_call(
        _flash_kernel,
        out_shape=jax.ShapeDtypeStruct((nb, t, H_B * V_B), BF16),
        grid=(nb, t // tq),
        in_specs=[pl.BlockSpec((1, H_B, tq, QX), lambda b, i: (b, 0, i, 0)),
                  pl.BlockSpec((1, H_B, t, QX), lambda b, i: (b, 0, 0, 0)),
                  pl.BlockSpec((1, t // tq, H_B * V_B, tq), lambda b, i: (b, 0, 0, 0)),
                  _const_spec((V_B, 1))],
        out_specs=pl.BlockSpec((1, tq, H_B * V_B), lambda b, i: (b, i, 0)),
        scratch_shapes=[pltpu.VMEM((1, rows), F32), pltpu.VMEM((1, rows), F32),
                        pltpu.VMEM((V_B, rows), F32)],
        compiler_params=_cparams(("arbitrary", "arbitrary")), name="flash",
    )(qx, kx, v_t, g_col)


def _decode_kernel(pt_ref, q_ref, knew_ref, wuv_ref, g_ref, lat_hbm, krT_hbm, o_ref,
                   lat_buf, kr_buf, s_scr, p_scr, sems, *, n_pages, td, ck):
    b = pl.program_id(0)
    slot = b % 2
    page = lat_hbm.shape[1]
    nck = (n_pages * page) // ck
    ppc = ck // page

    def page_copies(seq, sl, p):
        pg = pt_ref[seq * n_pages + p]
        rows = pl.ds(pl.multiple_of(p * page, page), page)
        return (pltpu.make_async_copy(lat_hbm.at[pg], lat_buf.at[sl, rows, :], sems.at[0, sl]),
                pltpu.make_async_copy(krT_hbm.at[pg], kr_buf.at[sl, p], sems.at[1, sl]))

    def fetch(seq, sl):
        def body(p, carry):
            for cp in page_copies(seq, sl, p):
                cp.start()
            return carry
        lax.fori_loop(0, n_pages, body, 0, unroll=8)

    @pl.when(b == 0)
    def _():
        fetch(0, 0)

    @pl.when(b + 1 < pl.num_programs(0))
    def _():
        fetch(b + 1, 1 - slot)

    pltpu.make_async_copy(lat_buf.at[slot], lat_buf.at[slot], sems.at[0, slot]).wait()
    pltpu.make_async_copy(kr_buf.at[slot], kr_buf.at[slot], sems.at[1, slot]).wait()

    q = q_ref[0].astype(BF16)
    ql, qr = q[:, 0:KV_LORA], q[:, KV_LORA:KV_LORA + QK_ROPE]

    def lat_chunk(c):
        return lat_buf[slot, c * ck:(c + 1) * ck, :].astype(BF16)

    def tree_sum(parts):
        while len(parts) > 1:
            parts = [parts[i] + parts[i + 1] if i + 1 < len(parts) else parts[i]
                     for i in range(0, len(parts), 2)]
        return parts[0]

    kn = jnp.concatenate([knew_ref[0], jnp.zeros((LANE - td, QCAT), F32)], axis=0).astype(BF16)
    s_new = _mm_nt(q, kn)
    t_q = lax.broadcasted_iota(jnp.int32, s_new.shape, 0) % td
    t_k = lax.broadcasted_iota(jnp.int32, s_new.shape, 1)
    s_new = jnp.where(t_k <= t_q, s_new, -jnp.inf)
    m = jnp.max(s_new, axis=-1, keepdims=True)
    p_new = jnp.exp2(s_new - m)
    l = jnp.sum(p_new, axis=-1, keepdims=True)
    acc = _mm(p_new.astype(BF16), kn[:, 0:KV_LORA])

    cps = nck // DECODE_SEGMENTS
    for seg in range(DECODE_SEGMENTS):
        chunks = range(seg * cps, (seg + 1) * cps)
        for c in chunks:
            s_rope = jnp.concatenate(
                [_mm(qr, kr_buf[slot, c * ppc + i].astype(BF16)) for i in range(ppc)], axis=1)
            s_scr[c] = _mm_nt(ql, lat_chunk(c)) + s_rope
        s_seg = s_scr[seg * cps:(seg + 1) * cps]
        m_new = jnp.maximum(m, jnp.max(jnp.max(s_seg, axis=0), axis=-1, keepdims=True))
        alpha = jnp.exp2(m - m_new)
        p_seg = jnp.exp2(s_seg - m_new[None])
        l = alpha * l + jnp.sum(jnp.sum(p_seg, axis=0), axis=-1, keepdims=True)
        p_scr[seg * cps:(seg + 1) * cps] = p_seg.astype(BF16)
        acc = alpha * acc + tree_sum([_mm(p_scr[c], lat_chunk(c)) for c in chunks])
        m = m_new
    o_lat = (acc / l).astype(BF16)
    ob = _mm(o_lat, wuv_ref[...])
    for hh in range(H_B):
        blk = ob[hh * td:(hh + 1) * td, hh * V_B:(hh + 1) * V_B]
        o_ref[0, :, hh * V_B:(hh + 1) * V_B] = _rms(blk, g_ref[...]).astype(o_ref.dtype)


def _decode(qcat, knew, wuv_all, g_out, cache_lat, cache_krT, page_table, *, ck):
    nb, _, td, _ = qcat.shape
    qcat = qcat.reshape(nb, H_B * td, QCAT)
    n_pages = page_table.shape[1]
    page = cache_lat.shape[1]
    past = n_pages * page
    rows = H_B * td
    grid_spec = pltpu.PrefetchScalarGridSpec(
        num_scalar_prefetch=1,
        grid=(nb,),
        in_specs=[pl.BlockSpec((1, rows, QCAT), lambda b, pt: (b, 0, 0)),
                  pl.BlockSpec((1, td, QCAT), lambda b, pt: (b, 0, 0)),
                  pl.BlockSpec(wuv_all.shape, lambda b, pt: (0, 0)),
                  pl.BlockSpec((1, V_B), lambda b, pt: (0, 0)),
                  pl.BlockSpec(memory_space=pl.ANY),
                  pl.BlockSpec(memory_space=pl.ANY)],
        out_specs=pl.BlockSpec((1, td, H_B * V_B), lambda b, pt: (b, 0, 0)),
        scratch_shapes=[pltpu.VMEM((2, past, KV_LORA), F32),
                        pltpu.VMEM((2, n_pages, QK_ROPE, page), F32),
                        pltpu.VMEM((past // ck, rows, ck), F32),
                        pltpu.VMEM((past // ck, rows, ck), BF16),
                        pltpu.SemaphoreType.DMA((2, 2))],
    )
    return pl.pallas_call(
        functools.partial(_decode_kernel, n_pages=n_pages, td=td, ck=ck),
        out_shape=jax.ShapeDtypeStruct((nb, td, H_B * V_B), F32),
        grid_spec=grid_spec,
        compiler_params=_cparams(("arbitrary",)), name="decode",
    )(page_table.reshape(-1), qcat, knew, wuv_all, g_out, cache_lat, cache_krT)


def _gelu_tanh(x):
    c = math.sqrt(2.0 / math.pi)
    hx = 0.5 * x
    return hx + hx * jnp.tanh(x * (c + (c * 0.044715) * (x * x)))


def _mixffn_kernel(oa_ref, ob_ref, x_ref, mod_ref, cst_ref, wo_ref, gpm_ref, gpf_ref,
                   wup_ref, wc_ref, bc_ref, wdn_ref, gpost_ref,
                   y_ref, cnew_ref, carry_ref, x1s_ref, h2s_ref):
    bb, tt, d = x_ref.shape
    m = bb * tt
    j = pl.program_id(1)
    assert bb == 1 or tt == SUBLANE
    mod = mod_ref[...]
    gt1, sh2, sc2, gt2 = (mod[:, :, k * d:(k + 1) * d] for k in (2, 3, 4, 5))

    def mix_tile(slot):
        mix = (_mm(oa_ref[...].reshape(m, HV).astype(BF16), wo_ref[0:HV, :])
               + _mm(ob_ref[...].reshape(m, H_B * V_B).astype(BF16), wo_ref[HV:HV + H_B * V_B, :]))
        x1 = x_ref[...] + gt1 * _rms(mix, gpm_ref[...]).reshape(bb, tt, d)
        x1s_ref[slot] = x1
        h2s_ref[slot] = (_rms(x1, gpf_ref[...]) * (1.0 + sc2) + sh2).reshape(m, d).astype(BF16)

    @pl.when(j == 0)
    def _():
        carry_ref[...] = cst_ref[...]
        mix_tile(0)

    @pl.when(j > 0)
    def _():
        x1 = x1s_ref[(j - 1) % 2]
        h2 = h2s_ref[(j - 1) % 2]
        y_ref[...] = x1 + gt2 * _rms(_conv_ffn(h2, bb, tt, wup_ref, wc_ref, bc_ref, wdn_ref, carry_ref, cnew_ref),
                                     gpost_ref[...]).reshape(bb, tt, d)
        mix_tile(j % 2)


def _conv_ffn(h2, bb, tt, wup_ref, wc_ref, bc_ref, wdn_ref, carry_ref, cnew_ref):
    m, d = h2.shape
    fc = FFN_CHUNK
    nc = wdn_ref.shape[0] // fc
    ng = m // SUBLANE
    sub = lax.broadcasted_iota(jnp.int32, (1, SUBLANE, fc), 1)

    def up_conv(k):
        cols = slice(k * fc, (k + 1) * fc)
        u = _mm(h2, wup_ref[:, cols])
        prev = carry_ref[:, :, cols]
        tail = u.reshape(bb, tt, fc)[:, tt - (CONV_W - 1):tt]
        carry_ref[:, :, cols] = tail
        cnew_ref[:, :, cols] = tail
        g3 = u.reshape(ng, SUBLANE, fc)
        r1, r2 = pltpu.roll(g3, 1, axis=1), pltpu.roll(g3, 2, axis=1)
        p1 = jnp.broadcast_to(prev[:, 1:2], (bb, SUBLANE, fc))
        p2 = jnp.where(sub == 0, jnp.broadcast_to(prev[:, 0:1], (bb, SUBLANE, fc)), p1)
        if bb == 1:
            p1 = jnp.concatenate([p1, r1[:ng - 1]], axis=0)
            p2 = jnp.concatenate([p2, r2[:ng - 1]], axis=0)
        u1 = jnp.where(sub < 1, p1, r1).reshape(m, fc)
        u2 = jnp.where(sub < 2, p2, r2).reshape(m, fc)
        wc = wc_ref[:, cols]
        return bc_ref[:, cols] + wc[2:3] * u + wc[1:2] * u1 + wc[0:1] * u2

    acc = jnp.zeros((m, d), F32)
    ready = [(up_conv(c), up_conv(nc + c)) for c in range(min(FFN_AHEAD, nc))]
    for c in range(nc):
        if c + FFN_AHEAD < nc:
            ready.append((up_conv(c + FFN_AHEAD), up_conv(nc + c + FFN_AHEAD)))
        val, gate = ready.pop(0)
        act = (val * _gelu_tanh(gate)).astype(BF16)
        acc = acc + _mm(act, wdn_ref[c * fc:(c + 1) * fc, :])
    return acc


def _mixffn(oa, ob, x, mod3, cstate, w, *, bb, tt):
    nb, t, d = x.shape
    f2 = cstate.shape[2]
    nt = t // tt
    grid = (nb // bb, nt + 1)
    tok_in = lambda width: pl.BlockSpec((bb, tt, width), lambda i, j: (i, jnp.minimum(j, nt - 1), 0))
    cs_spec = pl.BlockSpec((bb, CONV_W - 1, f2), lambda i, j: (i, 0, 0))
    return pl.pallas_call(
        _mixffn_kernel,
        out_shape=(jax.ShapeDtypeStruct((nb, t, d), F32),
                   jax.ShapeDtypeStruct((nb, CONV_W - 1, f2), F32)),
        grid=grid,
        in_specs=[tok_in(HV), tok_in(H_B * V_B), tok_in(d),
                  pl.BlockSpec((bb, 1, mod3.shape[2]), lambda i, j: (i, 0, 0)),
                  cs_spec,
                  _const_spec(w["wo"].shape), _const_spec((1, d)), _const_spec((1, d)),
                  _const_spec(w["wup"].shape), _const_spec(w["wconv"].shape), _const_spec(w["bconv"].shape),
                  _const_spec(w["wdn"].shape), _const_spec((1, d))],
        out_specs=(pl.BlockSpec((bb, tt, d), lambda i, j: (i, jnp.maximum(j - 1, 0), 0)), cs_spec),
        scratch_shapes=[pltpu.VMEM((bb, CONV_W - 1, f2), F32),
                        pltpu.VMEM((2, bb, tt, d), F32), pltpu.VMEM((2, bb * tt, d), BF16)],
        compiler_params=_cparams(("arbitrary", "arbitrary")), name="mixffn",
    )(oa, ob, x, mod3, cstate, w["wo"], w["g_post_mix"], w["g_pre_ffn"],
      w["wup"], w["wconv"], w["bconv"], w["wdn"], w["g_post_ffn"])


def _rot_cols(wr):
    half = wr.shape[-1] // 2
    return jnp.concatenate([-wr[..., half:], wr[..., :half]], axis=-1)


def _prep_weights(l, w_in, w_gla_a2, b_gla_a, g_gla_out, g_mla_q, g_mla_kv, w_mla_uq, w_mla_uk, w_mla_uv,
                  g_mla_out, w_o, w_ffn_up, w_ffn_conv, b_ffn_conv, w_ffn_down, g_pre_mix, g_post_mix,
                  g_pre_ffn, g_post_ffn):
    wi = w_in[l]
    d = wi.shape[0]
    o_gr = 2 * HK + 2 * HV
    o_cq = o_gr + GATE_RANK
    o_ckv = o_cq + Q_LORA
    o_kr = o_ckv + KV_LORA
    kr = wi[:, o_kr:o_kr + QK_ROPE]
    win = jnp.concatenate(
        [wi[:, 0:o_gr], wi[:, o_cq:o_ckv], wi[:, o_ckv:o_kr], kr, _rot_cols(kr),
         wi[:, o_gr:o_cq], jnp.zeros((d, LANE - GATE_RANK), wi.dtype)], axis=1).astype(BF16)
    wa2 = jnp.concatenate([w_gla_a2[l], jnp.zeros((LANE - GATE_RANK, HK), F32)], axis=0).astype(BF16)
    uq = w_mla_uq[l].reshape(Q_LORA, H_B, QK_NOPE + QK_ROPE)
    uq_rope = uq[:, :, QK_NOPE:]
    wuq = jnp.concatenate(
        [uq[:, :, :QK_NOPE].reshape(Q_LORA, H_B * QK_NOPE),
         jnp.concatenate([uq_rope, _rot_cols(uq_rope)], axis=-1).reshape(Q_LORA, H_B * 2 * QK_ROPE)],
        axis=1).astype(BF16)
    return dict(
        win=win, wa2=wa2, ba=b_gla_a[l].reshape(1, HK),
        g_gla_out=g_gla_out[l].reshape(1, DV_A),
        g_mla_q=g_mla_q[l].reshape(1, Q_LORA), g_mla_kv=g_mla_kv[l].reshape(1, KV_LORA),
        wuq=wuq, wukT=jnp.transpose(w_mla_uk[l], (1, 2, 0)).astype(BF16),
        wukx=w_mla_uk[l].reshape(KV_LORA, H_B * QK_NOPE).astype(BF16),
        wuvT=w_mla_uv[l].reshape(KV_LORA, H_B * V_B).T.astype(BF16),
        g_mla_outT=g_mla_out[l].reshape(V_B, 1),
        wuv_all=w_mla_uv[l].reshape(KV_LORA, H_B * V_B).astype(BF16),
        g_mla_out=g_mla_out[l].reshape(1, V_B),
        wo=w_o[l].astype(BF16),
        wup=w_ffn_up[l].astype(BF16), wconv=w_ffn_conv[l], bconv=b_ffn_conv[l].reshape(1, -1),
        wdn=w_ffn_down[l].astype(BF16),
        g_pre_mix=g_pre_mix[l].reshape(1, d), g_post_mix=g_post_mix[l].reshape(1, d),
        g_pre_ffn=g_pre_ffn[l].reshape(1, d), g_post_ffn=g_post_ffn[l].reshape(1, d),
    )


def _rope_table(pos):
    inv = ROPE_BASE ** (-jnp.arange(0, QK_ROPE, 2, dtype=F32) / QK_ROPE)
    ang = pos.astype(F32)[:, None] * inv[None, :]
    cos, sin = jnp.cos(ang), jnp.sin(ang)
    return jnp.concatenate([cos, cos, sin, sin], axis=-1)


def _token_tile(t, target):
    tt = min(t, target)
    while t % tt:
        tt //= 2
    return tt


def _layer(x, mod3, cs, w, gla_s0, conv_state, *, per_seq, attend):
    nb, t, d = x.shape
    if per_seq:
        bb, tt = nb, t
    else:
        bb, tt = 1, _token_tile(t, 512)
    act_dtype = F32 if per_seq else BF16
    tkb = _token_tile(bb * tt, ATTN_TILE)
    qa, ka, va, ra, la, *mla, lat, kr = _inproj(
        x, mod3, cs, w, bb=bb, tt=tt, tkb=tkb, act_dtype=act_dtype, expanded=not per_seq)

    if per_seq:
        o_a, s_fin = _gla(qa, ka, la, va, ra, gla_s0, w["g_gla_out"], chunk=GLA_CHUNK_SHORT,
                          sub=GLA_CHUNK_SHORT, nseq=_token_tile(nb, 8), tc=t, out_dtype=act_dtype)
    else:
        o_a, s_fin = _gla(qa, ka, la, va, ra, gla_s0, w["g_gla_out"], chunk=LANE, sub=LANE // 4,
                          nseq=1, tc=_token_tile(t, 512), out_dtype=act_dtype)

    if not per_seq:
        kr = jnp.swapaxes(kr, 1, 2)
    o_b = attend(*mla)
    y, c_new = _mixffn(o_a, o_b, x, mod3, conv_state, w, bb=bb, tt=tt if per_seq else _token_tile(t, FFN_ROWS))
    return y, lat, kr, s_fin, c_new


def kernel(x_prompt, x_sample, c_prompt, c_sample, cache_latent, cache_krope, state_gla, state_conv, page_table, w_ada, b_ada, g_pre_mix, g_post_mix, g_pre_ffn, g_post_ffn, w_in, w_gla_a2, b_gla_a, g_gla_out, g_mla_q, g_mla_kv, w_mla_uq, w_mla_uk, w_mla_uv, g_mla_out, w_o, w_ffn_up, w_ffn_conv, b_ffn_conv, w_ffn_down):
    depth = w_ada.shape[0]
    nbp, tp, d = x_prompt.shape
    nbs, ts, _ = x_sample.shape
    n_pages, page = page_table.shape[1], cache_latent.shape[2]
    past_len = n_pages * page
    f2 = w_ffn_up.shape[2]
    cs_p = _rope_table(jnp.arange(tp, dtype=jnp.int32))
    cs_s = _rope_table(past_len + jnp.arange(ts, dtype=jnp.int32))

    hp, hs = x_prompt, x_sample
    outs = [[] for _ in range(8)]
    for l in range(depth):
        w = _prep_weights(l, w_in, w_gla_a2, b_gla_a, g_gla_out, g_mla_q, g_mla_kv, w_mla_uq, w_mla_uk,
                          w_mla_uv, g_mla_out, w_o, w_ffn_up, w_ffn_conv, b_ffn_conv, w_ffn_down,
                          g_pre_mix, g_post_mix, g_pre_ffn, g_post_ffn)
        mod = _ada(jnp.concatenate([c_prompt, c_sample], axis=0), w_ada[l], b_ada[l])
        mod3 = mod.reshape(nbp + nbs, 1, mod.shape[1])

        def attend_prompt(qx, kx, v_t):
            return _flash(qx, kx, v_t, w["g_mla_outT"], tq=_token_tile(tp, ATTN_TILE))

        def attend_sample(qcat, kcat):
            return _decode(qcat, kcat, w["wuv_all"], w["g_mla_out"], cache_latent[l],
                           jnp.swapaxes(cache_krope[l], 1, 2), page_table, ck=8 * page)

        hp, a1, a2, a3, a4 = _layer(
            hp, mod3[:nbp], cs_p, w, jnp.zeros((nbp, HK, DV_A), F32),
            jnp.zeros((nbp, CONV_W - 1, f2), F32), per_seq=False, attend=attend_prompt)
        hs, b1, b2, b3, b4 = _layer(
            hs, mod3[nbp:], cs_s, w, state_gla[l].reshape(nbs, HK, DV_A), state_conv[l],
            per_seq=True, attend=attend_sample)
        for lst, val in zip(outs, (a1, a2, a3.reshape(nbp, H_A, DK_A, DV_A), a4,
                                   b1, b2, b3.reshape(nbs, H_A, DK_A, DV_A), b4)):
            lst.append(val)
    return (hp, hs) + tuple(jnp.stack(o) for o in outs)
```
